```python
import math
import jax, jax.numpy as jnp
from jax import lax
import numpy as np

D_MODEL = 2048
BATCH = 2
SEQ = 4096
DEPTH = 4
DEC_BATCH = 8
DEC_SEQ = 1
PAST_LEN = 16384
PAGE_SIZE = 128

H_A = 8
DH_A = 128
DV_A = 2 * DH_A
D_A = H_A * DV_A
H_B = 4
DK_B = 256
DV_B = 512
D_B = H_B * DV_B
GATE_RANK = 16
GATE_TAU = 16.0
GLA_CHUNK = 64
D_FF = -(-(8 * D_MODEL) // (3 * 256)) * 256
PLE_DIM = 256
N_BUCKETS = 32
MAX_EXACT = 16
MAX_DISTANCE = 128
Q_BLOCK = 128
EPS = 1e-6

kernel_name = 'hybrid_diffattn_gla_decode_step'


def rmsnorm(x, g):
    xf = x.astype(jnp.float32)
    y = xf * lax.rsqrt(jnp.mean(xf * xf, axis=-1, keepdims=True) + EPS)
    return (y * g.astype(jnp.float32)).astype(x.dtype)


def rel_bucket(dist):
    n = jnp.maximum(dist, 0)
    nf = jnp.maximum(n, 1).astype(jnp.float32)
    large = MAX_EXACT + (jnp.log(nf / MAX_EXACT) / math.log(MAX_DISTANCE / MAX_EXACT)
                         * (N_BUCKETS - MAX_EXACT)).astype(jnp.int32)
    large = jnp.minimum(large, N_BUCKETS - 1)
    return jnp.where(n < MAX_EXACT, n, large)


def diff_attend(q, k, v, q_pos, k_pos, rel_bias, lam):
    s = jnp.einsum('bqhcd,bkhcd->bhcqk', q, k).astype(jnp.float32) * (DH_A ** -0.5)
    dist = q_pos[:, None] - k_pos[None, :]
    bias = jnp.transpose(rel_bias[rel_bucket(dist)], (2, 0, 1)).astype(jnp.float32)
    s = jnp.where(dist >= 0, s + bias[None, :, None], -jnp.inf)
    a = jax.nn.softmax(s, axis=-1)
    w = a[:, :, 0] - lam * a[:, :, 1]
    return jnp.einsum('bhqk,bkhd->bqhd', w.astype(v.dtype), v)


def diff_attention(q, k, v, q0, rel_bias, lam):
    B, L = q.shape[:2]
    qb = math.gcd(L, Q_BLOCK)
    nb = L // qb
    q_blocks = jnp.moveaxis(q.reshape(B, nb, qb, H_A, 2, DH_A), 1, 0)
    k_pos = jnp.arange(k.shape[1], dtype=jnp.int32)

    def block(args):
        qi, bi = args
        q_pos = q0 + bi * qb + jnp.arange(qb, dtype=jnp.int32)
        return diff_attend(qi, k, v, q_pos, k_pos, rel_bias, lam)

    out = lax.map(block, (q_blocks, jnp.arange(nb, dtype=jnp.int32)))
    return jnp.moveaxis(out, 0, 1).reshape(B, L, H_A, DV_A)


def gla_recurrence(q, k, v, log_a, s0):
    B, L = q.shape[:2]
    c = math.gcd(L, GLA_CHUNK)
    nc = L // c

    def chunks(t):
        return jnp.transpose(t.reshape(B, nc, c, H_B, t.shape[-1]).astype(jnp.float32), (1, 0, 3, 2, 4))

    causal = jnp.tril(jnp.ones((c, c), dtype=bool))[:, :, None]

    def step(s, inp):
        qi, ki, vi, gi = inp
        b = jnp.cumsum(gi, axis=2)
        o_inter = jnp.einsum('bhtd,bhde->bhte', qi * jnp.exp(b), s)
        decay = jnp.exp(jnp.where(causal, b[:, :, :, None, :] - b[:, :, None, :, :], -jnp.inf))
        attn = jnp.einsum('bhtd,bhsd,bhtsd->bhts', qi, ki, decay)
        o_intra = jnp.einsum('bhts,bhse->bhte', attn, vi)
        b_last = b[:, :, -1:, :]
        s_new = jnp.exp(b_last[:, :, 0, :, None]) * s + jnp.einsum('bhsd,bhse->bhde', ki * jnp.exp(b_last - b), vi)
        return s_new, o_inter + o_intra

    s_fin, o = lax.scan(step, s0.astype(jnp.float32), (chunks(q), chunks(k), chunks(v), chunks(log_a)))
    o = jnp.transpose(o, (1, 0, 3, 2, 4)).reshape(B, L, H_B, DV_B)
    return o.astype(v.dtype), s_fin


def mixer_inputs(x, lw):
    B, L, _ = x.shape
    h = rmsnorm(x, lw['norm_mix_g'])
    z = h @ lw['w_in']
    sizes = [H_A * 2 * DH_A, H_A * 2 * DH_A, D_A, H_B * DK_B, H_B * DK_B, D_B, D_B, GATE_RANK, D_MODEL, D_MODEL]
    qa, ka, va, qb, kb, vb, r, g_low, ga, gb = jnp.split(z, np.cumsum(sizes)[:-1].tolist(), axis=-1)
    qa = rmsnorm(qa.reshape(B, L, H_A, 2, DH_A), lw['q_norm_g'])
    ka = rmsnorm(ka.reshape(B, L, H_A, 2, DH_A), lw['k_norm_g'])
    va = va.reshape(B, L, H_A, DV_A)
    qb = qb.reshape(B, L, H_B, DK_B) * (DK_B ** -0.5)
    kb = kb.reshape(B, L, H_B, DK_B)
    vb = vb.reshape(B, L, H_B, DV_B)
    log_a = (jax.nn.log_sigmoid((g_low @ lw['gla_gate_w2'] + lw['gla_gate_b']).astype(jnp.float32))
             / GATE_TAU).reshape(B, L, H_B, DK_B)
    return qa, ka, va, qb, kb, vb, r, log_a, ga, gb


def trunk_layer(x, p_i, gla_s0, k_past, v_past, q0, rel_bias, lam, lam_init, lw):
    B, L, _ = x.shape
    qa, ka, va, qb, kb, vb, r, log_a, ga, gb = mixer_inputs(x, lw)
    k_all = ka if k_past is None else jnp.concatenate([k_past.astype(ka.dtype), ka], axis=1)
    v_all = va if v_past is None else jnp.concatenate([v_past.astype(va.dtype), va], axis=1)
    ya = diff_attention(qa, k_all, v_all, q0, rel_bias, lam)
    ya = (rmsnorm(ya, lw['diff_norm_g']) * (1.0 - lam_init)).reshape(B, L, D_A)
    yb, gla_s = gla_recurrence(qb, kb, vb, log_a, gla_s0)
    yb = (rmsnorm(yb, lw['gla_norm_g']) * jax.nn.silu(r.reshape(B, L, H_B, DV_B))).reshape(B, L, D_B)
    merged = jax.nn.sigmoid(ga) * (ya @ lw['w_pa']) + jax.nn.sigmoid(gb) * (yb @ lw['w_pb'])
    x = x + merged @ lw['w_o']
    h = rmsnorm(x, lw['norm_ffn_g'])
    x = x + (jax.nn.silu(h @ lw['w_gate']) * (h @ lw['w_up'])) @ lw['w_down']
    h = rmsnorm(x, lw['norm_ple_g'])
    x = x + jax.nn.sigmoid(h @ lw['w_ple_gate']) * (p_i @ lw['w_ple_proj'])
    return x, ka.reshape(B, L, H_A, 2 * DH_A), va, gla_s


def setup_inputs(seed: int = 0) -> dict:
    key = jax.random.key(seed)
    ks = iter(jax.random.split(key, 40))

    def nrm(shape, scale):
        return scale * jax.random.normal(next(ks), shape, jnp.float32)

    def gain(shape):
        return 1.0 + nrm(shape, 0.05)

    n_pages = PAST_LEN // PAGE_SIZE
    n_used = DEC_BATCH * n_pages
    n_pool = n_used + max(1, n_used // 4)
    in_cols = 2 * H_A * 2 * DH_A + D_A + 2 * H_B * DK_B + 2 * D_B + GATE_RANK + 2 * D_MODEL
    page_table = jax.random.permutation(next(ks), n_pool)[:n_used].reshape(DEC_BATCH, n_pages).astype(jnp.int32)
    return {
        'x_prompt': nrm((BATCH, SEQ, D_MODEL), 1.0),
        'x_sample': nrm((DEC_BATCH, DEC_SEQ, D_MODEL), 1.0),
        'cache_k': nrm((DEPTH, n_pool, PAGE_SIZE, H_A, 2 * DH_A), 1.0),
        'cache_v': nrm((DEPTH, n_pool, PAGE_SIZE, H_A, DV_A), 1.0),
        'state_gla': nrm((DEPTH, DEC_BATCH, H_B, DK_B, DV_B), 0.2),
        'page_table': page_table,
        'p_prompt': nrm((DEPTH, BATCH, SEQ, PLE_DIM), 1.0),
        'p_sample': nrm((DEPTH, DEC_BATCH, DEC_SEQ, PLE_DIM), 1.0),
        'rel_bias': nrm((N_BUCKETS, H_A), 0.5),
        'norm_mix_g': gain((DEPTH, D_MODEL)),
        'w_in': nrm((DEPTH, D_MODEL, in_cols), D_MODEL ** -0.5),
        'gla_gate_w2': nrm((DEPTH, GATE_RANK, H_B * DK_B), GATE_RANK ** -0.5),
        'gla_gate_b': nrm((DEPTH, H_B * DK_B), 0.1),
        'q_norm_g': gain((DEPTH, DH_A)),
        'k_norm_g': gain((DEPTH, DH_A)),
        'lambda_q1': nrm((DEPTH, DH_A), 0.1),
        'lambda_k1': nrm((DEPTH, DH_A), 0.1),
        'lambda_q2': nrm((DEPTH, DH_A), 0.1),
        'lambda_k2': nrm((DEPTH, DH_A), 0.1),
        'diff_norm_g': gain((DEPTH, DV_A)),
        'gla_norm_g': gain((DEPTH, DV_B)),
        'w_pa': nrm((DEPTH, D_A, D_MODEL), D_A ** -0.5),
        'w_pb': nrm((DEPTH, D_B, D_MODEL), D_B ** -0.5),
        'w_o': nrm((DEPTH, D_MODEL, D_MODEL), D_MODEL ** -0.5),
        'norm_ffn_g': gain((DEPTH, D_MODEL)),
        'w_gate': nrm((DEPTH, D_MODEL, D_FF), D_MODEL ** -0.5),
        'w_up': nrm((DEPTH, D_MODEL, D_FF), D_MODEL ** -0.5),
        'w_down': nrm((DEPTH, D_FF, D_MODEL), D_FF ** -0.5),
        'norm_ple_g': gain((DEPTH, D_MODEL)),
        'w_ple_gate': nrm((DEPTH, D_MODEL, D_MODEL), D_MODEL ** -0.5),
        'w_ple_proj': nrm((DEPTH, PLE_DIM, D_MODEL), PLE_DIM ** -0.5),
    }


def reference(x_prompt, x_sample, cache_k, cache_v, state_gla, page_table, p_prompt, p_sample,
              rel_bias, norm_mix_g, w_in, gla_gate_w2, gla_gate_b, q_norm_g, k_norm_g,
              lambda_q1, lambda_k1, lambda_q2, lambda_k2, diff_norm_g, gla_norm_g,
              w_pa, w_pb, w_o, norm_ffn_g, w_gate, w_up, w_down, norm_ple_g, w_ple_gate, w_ple_proj):
    dec_b, n_pages = page_table.shape
    past_len = n_pages * cache_k.shape[2]
    xp, xs = x_prompt, x_sample
    gla0_prompt = jnp.zeros((x_prompt.shape[0], H_B, DK_B, DV_B), jnp.float32)
    kp_l, vp_l, sp_l, ks_l, vs_l, ss_l = [], [], [], [], [], []
    for i in range(DEPTH):
        lw = {
            'norm_mix_g': norm_mix_g[i], 'w_in': w_in[i], 'gla_gate_w2': gla_gate_w2[i],
            'gla_gate_b': gla_gate_b[i], 'q_norm_g': q_norm_g[i], 'k_norm_g': k_norm_g[i],
            'diff_norm_g': diff_norm_g[i], 'gla_norm_g': gla_norm_g[i], 'w_pa': w_pa[i],
            'w_pb': w_pb[i], 'w_o': w_o[i], 'norm_ffn_g': norm_ffn_g[i], 'w_gate': w_gate[i],
            'w_up': w_up[i], 'w_down': w_down[i], 'norm_ple_g': norm_ple_g[i],
            'w_ple_gate': w_ple_gate[i], 'w_ple_proj': w_ple_proj[i],
        }
        lam_init = 0.8 - 0.6 * math.exp(-0.3 * i)
        lam = (jnp.exp(jnp.sum(lambda_q1[i].astype(jnp.float32) * lambda_k1[i].astype(jnp.float32)))
               - jnp.exp(jnp.sum(lambda_q2[i].astype(jnp.float32) * lambda_k2[i].astype(jnp.float32)))
               + lam_init)
        xp, kp, vp, sp = trunk_layer(xp, p_prompt[i], gla0_prompt, None, None, 0, rel_bias, lam, lam_init, lw)
        k_past = cache_k[i, page_table].reshape(dec_b, past_len, H_A, 2, DH_A)
        v_past = cache_v[i, page_table].reshape(dec_b, past_len, H_A, DV_A)
        xs, ksm, vsm, ssm = trunk_layer(xs, p_sample[i], state_gla[i], k_past, v_past, past_len,
                                        rel_bias, lam, lam_init, lw)
        kp_l.append(kp); vp_l.append(vp); sp_l.append(sp)
        ks_l.append(ksm); vs_l.append(vsm); ss_l.append(ssm)
    k_prompt = jnp.stack(kp_l)
    v_prompt = jnp.stack(vp_l)
    gla_prompt = jnp.stack(sp_l)
    k_sample = jnp.stack(ks_l)
    v_sample = jnp.stack(vs_l)
    gla_sample = jnp.stack(ss_l)
    return (xp, xs, k_prompt, v_prompt, gla_prompt, k_sample, v_sample, gla_sample)
```

```python
import functools
import math

import numpy as np
import jax
import jax.numpy as jnp
from jax import lax
from jax.experimental import pallas as pl
from jax.experimental.pallas import tpu as pltpu

EPS = 1e-6
GATE_TAU = 16.0
GLA_CHUNK = 64
MAX_EXACT = 16
MAX_DISTANCE = 128
LANES = 128
NEG = -1e30
VMEM_LIMIT = 48 * 1024 * 1024
ATTN_TILE = 512
GLA_BLOCK = 512

BF16 = jnp.bfloat16
F32 = jnp.float32


def _cparams(sem):
    return pltpu.CompilerParams(dimension_semantics=sem, vmem_limit_bytes=VMEM_LIMIT)


def _tile(n, pref):
    if n <= pref:
        return n
    t = pref
    while n % t:
        t //= 2
    return t


def _rmsnorm_kernel(x_ref, g_ref, o_ref):
    x = x_ref[...]
    ms = jnp.mean(x * x, axis=-1, keepdims=True)
    o_ref[...] = (x * lax.rsqrt(ms + EPS) * g_ref[...]).astype(o_ref.dtype)


def rmsnorm_rows(x, g):
    m, d = x.shape
    tm = _tile(m, 512)
    return pl.pallas_call(
        _rmsnorm_kernel,
        grid=(m // tm,),
        in_specs=[pl.BlockSpec((tm, d), lambda i: (i, 0)),
                  pl.BlockSpec((1, d), lambda i: (0, 0))],
        out_specs=pl.BlockSpec((tm, d), lambda i: (i, 0)),
        out_shape=jax.ShapeDtypeStruct((m, d), BF16),
        compiler_params=_cparams(("parallel",)),
        name="rmsnorm",
    )(x, g.reshape(1, d).astype(F32))


def _mm_kernel(*refs, pair_a, n_a, n_extra, epilogue):
    a_refs = refs[:n_a]
    w_refs = refs[n_a:n_a + len(pair_a)]
    extras = refs[n_a + len(pair_a):n_a + len(pair_a) + n_extra]
    outs = refs[n_a + len(pair_a) + n_extra:]
    a_vals = [a[...].astype(BF16) for a in a_refs]
    accs = [jnp.dot(a_vals[ai], w[...], preferred_element_type=F32)
            for ai, w in zip(pair_a, w_refs)]
    epilogue(accs, extras, outs)


def fused_matmul(a_list, pairs, extras, outs, epilogue, *, tm=1024, tn=1024, name="mm"):
    m = a_list[0].shape[0]
    n = pairs[0][1].shape[1]
    tm = _tile(m, tm)
    tn = _tile(n, tn)
    grid = (m // tm, n // tn)
    in_specs, args = [], []
    for a in a_list:
        in_specs.append(pl.BlockSpec((tm, a.shape[1]), lambda i, j: (i, 0)))
        args.append(a)
    for _, w in pairs:
        in_specs.append(pl.BlockSpec((w.shape[0], tn), lambda i, j: (0, j)))
        args.append(w)
    for arr, kind, off in extras:
        if kind == "tile":
            in_specs.append(pl.BlockSpec((tm, tn), lambda i, j, off=off: (i, off + j)))
        elif kind == "row":
            in_specs.append(pl.BlockSpec((1, tn), lambda i, j: (0, j)))
        else:
            in_specs.append(pl.BlockSpec(arr.shape, lambda i, j, nd=arr.ndim: (0,) * nd))
        args.append(arr)
    out_specs, out_shape = [], []
    for width, dt in outs:
        if width == n:
            out_specs.append(pl.BlockSpec((tm, tn), lambda i, j: (i, j)))
        else:
            assert grid[1] == 1
            out_specs.append(pl.BlockSpec((tm, width), lambda i, j: (i, 0)))
        out_shape.append(jax.ShapeDtypeStruct((m, width), dt))
    kern = functools.partial(_mm_kernel, pair_a=tuple(p[0] for p in pairs), n_a=len(a_list),
                             n_extra=len(extras), epilogue=epilogue)
    res = pl.pallas_call(
        kern, grid=grid, in_specs=in_specs, out_specs=out_specs, out_shape=out_shape,
        compiler_params=_cparams(("parallel", "arbitrary")), name=name,
    )(*args)
    return res


def _ep_cast(accs, extras, outs):
    for o in outs:
        o[...] = accs[0].astype(o.dtype)


def _ep_groupnorm(accs, extras, outs, *, group, scale):
    acc = accs[0]
    g = extras[0][...] * scale
    for s in range(acc.shape[1] // group):
        z = acc[:, s * group:(s + 1) * group]
        ms = jnp.mean(z * z, axis=-1, keepdims=True)
        y = z * lax.rsqrt(ms + EPS) * g
        for o in outs:
            o[:, s * group:(s + 1) * group] = y.astype(o.dtype)


def _ep_gate(accs, extras, outs):
    y = jnp.dot(accs[0], extras[0][...], preferred_element_type=F32,
                precision=lax.Precision.HIGHEST) + extras[1][...]
    ls = jnp.minimum(y, 0.0) - jnp.log1p(jnp.exp(-jnp.abs(y)))
    outs[0][...] = ls * (1.0 / GATE_TAU)


def _ep_merge(accs, extras, outs):
    ga = extras[0][...].astype(F32)
    gb = extras[1][...].astype(F32)
    outs[0][...] = (jax.nn.sigmoid(ga) * accs[0] + jax.nn.sigmoid(gb) * accs[1]).astype(outs[0].dtype)


def _ep_residual(accs, extras, outs):
    outs[0][...] = extras[0][...] + accs[0]


def _ep_swiglu(accs, extras, outs):
    g = accs[0]
    outs[0][...] = (g * jax.nn.sigmoid(g) * accs[1]).astype(outs[0].dtype)


def _ep_ple(accs, extras, outs):
    outs[0][...] = extras[0][...] + jax.nn.sigmoid(accs[0]) * accs[1]


def _bucket_table(n, n_buckets):
    d = np.arange(n)
    nf = np.maximum(d, 1).astype(np.float64)
    large = MAX_EXACT + (np.log(nf / MAX_EXACT) / math.log(MAX_DISTANCE / MAX_EXACT)
                         * (n_buckets - MAX_EXACT)).astype(np.int64)
    large = np.minimum(large, n_buckets - 1)
    return np.where(d < MAX_EXACT, d, large).astype(np.int32)


def _attn_kernel(qi_tab, ki_tab, far_ref, q_ref, k_ref, v_ref, bias_ref, lam_ref, g_ref, o_ref,
                 m_sc, l_sc, acc_sc, *, lam_init, dh):
    h = pl.program_id(1)
    p = pl.program_id(2)
    qi = qi_tab[p]
    ki = ki_tab[p]

    @pl.when(ki == 0)
    def _():
        m_sc[...] = jnp.full(m_sc.shape, NEG, F32)
        l_sc[...] = jnp.zeros(l_sc.shape, F32)
        acc_sc[...] = jnp.zeros(acc_sc.shape, F32)

    def step(bias):
        v = v_ref[0]
        for c in range(2):
            qc = q_ref[0, :, c * dh:(c + 1) * dh]
            kc = k_ref[0, :, c * dh:(c + 1) * dh]
            s = lax.dot_general(qc, kc, (((1,), (1,)), ((), ())), preferred_element_type=F32) + bias
            m_prev = m_sc[c]
            m_new = jnp.maximum(m_prev, jnp.max(s, axis=-1, keepdims=True))
            alpha = jnp.exp(m_prev - m_new)
            pr = jnp.exp(s - m_new)
            l_sc[c] = alpha * l_sc[c] + jnp.sum(pr, axis=-1, keepdims=True)
            acc_sc[c] = alpha * acc_sc[c] + jnp.dot(pr.astype(BF16), v, preferred_element_type=F32)
            m_sc[c] = m_new

    @pl.when(ki == qi)
    def _():
        step(bias_ref[0, 0])

    @pl.when(ki == qi - 1)
    def _():
        step(bias_ref[0, 1])

    @pl.when(ki < qi - 1)
    def _():
        step(far_ref[h])

    @pl.when(ki == qi)
    def _():
        lv = lam_ref[...]
        lam = (jnp.exp(jnp.sum(lv[0:1] * lv[1:2], axis=-1, keepdims=True))
               - jnp.exp(jnp.sum(lv[2:3] * lv[3:4], axis=-1, keepdims=True)) + lam_init)
        o = acc_sc[0] / l_sc[0] - lam * (acc_sc[1] / l_sc[1])
        ms = jnp.mean(o * o, axis=-1, keepdims=True)
        y = o * lax.rsqrt(ms + EPS) * g_ref[...] * (1.0 - lam_init)
        o_ref[0] = y.astype(o_ref.dtype)


def prompt_attention(q, k, v, rel_bias, lam_vecs, diff_g, lam_init, n_heads):
    b, l, dtot = q.shape
    dv = dtot // n_heads
    dh = dv // 2
    t = _tile(l, ATTN_TILE)
    nq = l // t
    pairs = [(qi, ki) for qi in range(nq) for ki in range(qi + 1)]
    qi_tab = jnp.asarray([p[0] for p in pairs], jnp.int32)
    ki_tab = jnp.asarray([p[1] for p in pairs], jnp.int32)
    n_buckets = rel_bias.shape[0]
    table = _bucket_table(2 * t, n_buckets)
    bd = rel_bias.astype(F32)[table]
    ii = np.arange(t)[:, None] - np.arange(t)[None, :]
    diag = jnp.where(jnp.asarray(ii >= 0)[:, :, None], bd[np.maximum(ii, 0)], NEG)
    sub = bd[np.minimum(ii + t, 2 * t - 1)]
    bias = jnp.transpose(jnp.stack([diag, sub]), (3, 0, 1, 2))
    far = bd[2 * t - 1]
    assert t + 1 >= MAX_DISTANCE

    kern = functools.partial(_attn_kernel, lam_init=lam_init, dh=dh)
    grid_spec = pltpu.PrefetchScalarGridSpec(
        num_scalar_prefetch=2,
        grid=(b, n_heads, len(pairs)),
        in_specs=[
            pl.BlockSpec(memory_space=pltpu.SMEM),
            pl.BlockSpec((1, t, dv), lambda bi, h, p, qt, kt: (bi, qt[p], h)),
            pl.BlockSpec((1, t, dv), lambda bi, h, p, qt, kt: (bi, kt[p], h)),
            pl.BlockSpec((1, t, dv), lambda bi, h, p, qt, kt: (bi, kt[p], h)),
            pl.BlockSpec((1, 2, t, t), lambda bi, h, p, qt, kt: (h, 0, 0, 0)),
            pl.BlockSpec((4, dh), lambda bi, h, p, qt, kt: (0, 0)),
            pl.BlockSpec((1, dv), lambda bi, h, p, qt, kt: (0, 0)),
        ],
        out_specs=pl.BlockSpec((1, t, dv), lambda bi, h, p, qt, kt: (bi, qt[p], h)),
        scratch_shapes=[pltpu.VMEM((2, t, 1), F32), pltpu.VMEM((2, t, 1), F32),
                        pltpu.VMEM((2, t, dv), F32)],
    )
    return pl.pallas_call(
        kern, grid_spec=grid_spec,
        out_shape=jax.ShapeDtypeStruct((b, l, dtot), BF16),
        compiler_params=_cparams(("parallel", "parallel", "arbitrary")),
        name="diff_attn_prompt",
    )(qi_tab, ki_tab, far, q, k, v, bias, lam_vecs, diff_g.reshape(1, dv).astype(F32))


def _gla_kernel(q_ref, k_ref, v_ref, r_ref, g_ref, gn_ref, y_ref, s_ref, st_sc, *, chunk, nchunk, qscale):
    li = pl.program_id(2)

    @pl.when(li == 0)
    def _():
        st_sc[...] = jnp.zeros(st_sc.shape, F32)

    row = lax.broadcasted_iota(jnp.int32, (chunk, chunk), 0)
    col = lax.broadcasted_iota(jnp.int32, (chunk, chunk), 1)
    causal = row >= col
    tril = causal.astype(F32)
    gn = gn_ref[...]

    def body(c, carry):
        sl = pl.ds(pl.multiple_of(c * chunk, chunk), chunk)
        g = g_ref[0, sl, :]
        bcum = jnp.dot(tril, g, preferred_element_type=F32, precision=lax.Precision.HIGHEST)
        q = q_ref[0, sl, :].astype(F32) * qscale
        k = k_ref[0, sl, :].astype(F32)
        v = v_ref[0, sl, :]
        qt = (q * jnp.exp(bcum)).astype(BF16)
        kt = (k * jnp.exp(-bcum)).astype(BF16)
        st = st_sc[...]
        o = lax.dot_general(qt, st.astype(BF16), (((1,), (1,)), ((), ())), preferred_element_type=F32)
        a = lax.dot_general(qt, kt, (((1,), (1,)), ((), ())), preferred_element_type=F32)
        a = jnp.where(causal, a, 0.0)
        o = o + jnp.dot(a.astype(BF16), v, preferred_element_type=F32)
        b_last = bcum[chunk - 1:chunk, :]
        k2 = (k * jnp.exp(b_last - bcum)).astype(BF16)
        st_sc[...] = st * jnp.exp(b_last) + lax.dot_general(
            v, k2, (((0,), (0,)), ((), ())), preferred_element_type=F32)
        ms = jnp.mean(o * o, axis=-1, keepdims=True)
        rr = r_ref[0, sl, :].astype(F32)
        y = o * lax.rsqrt(ms + EPS) * gn * (rr * jax.nn.sigmoid(rr))
        y_ref[0, sl, :] = y.astype(y_ref.dtype)
        return carry

    lax.fori_loop(0, nchunk, body, 0)

    @pl.when(li == pl.num_programs(2) - 1)
    def _():
        s_ref[0, 0] = st_sc[...]


def prompt_gla(z, log_a, gla_g, n_heads, dk, dv, off_q, off_k, off_v, off_r):
    b, l, _ = z.shape
    chunk = math.gcd(l, GLA_CHUNK)
    lb = _tile(l, GLA_BLOCK)
    nchunk = lb // chunk
    kern = functools.partial(_gla_kernel, chunk=chunk, nchunk=nchunk, qscale=dk ** -0.5)
    oq, ok, ov, orr = off_q // dk, off_k // dk, off_v // dv, off_r // dv
    return pl.pallas_call(
        kern,
        grid=(b, n_heads, l // lb),
        in_specs=[
            pl.BlockSpec((1, lb, dk), lambda bi, h, i: (bi, i, oq + h)),
            pl.BlockSpec((1, lb, dk), lambda bi, h, i: (bi, i, ok + h)),
            pl.BlockSpec((1, lb, dv), lambda bi, h, i: (bi, i, ov + h)),
            pl.BlockSpec((1, lb, dv), lambda bi, h, i: (bi, i, orr + h)),
            pl.BlockSpec((1, lb, dk), lambda bi, h, i: (bi, i, h)),
            pl.BlockSpec((1, dv), lambda bi, h, i: (0, 0)),
        ],
        out_specs=[
            pl.BlockSpec((1, lb, dv), lambda bi, h, i: (bi, i, h)),
            pl.BlockSpec((1, 1, dv, dk), lambda bi, h, i: (bi, h, 0, 0)),
        ],
        out_shape=[jax.ShapeDtypeStruct((b, l, n_heads * dv), BF16),
                   jax.ShapeDtypeStruct((b, n_heads, dv, dk), F32)],
        scratch_shapes=[pltpu.VMEM((dv, dk), F32)],
        compiler_params=_cparams(("parallel", "parallel", "arbitrary")),
        name="gla_prompt",
    )(z, z, z, z, log_a, gla_g.reshape(1, dv).astype(F32))


def _decode_attn_kernel(pt_ref, q_ref, *refs, n_sub, n_heads, lam_init):
    k_refs = refs[:n_sub]
    v_refs = refs[n_sub:2 * n_sub]
    (bias_last_ref, far_ref, own_ref, kn_ref, vn_ref, lam_ref, g_ref, o_ref,
     m_sc, l_sc, acc_sc) = refs[2 * n_sub:]
    s_idx = pl.program_id(1)
    last = pl.num_programs(1) - 1

    @pl.when(s_idx == 0)
    def _():
        m_sc[...] = jnp.full(m_sc.shape, NEG, F32)
        l_sc[...] = jnp.zeros(l_sc.shape, F32)
        acc_sc[...] = jnp.zeros(acc_sc.shape, F32)

    q = q_ref[0]
    nrow = q.shape[0]

    def update(kmat, vmat, bias):
        r = kmat.shape[0]
        s = lax.dot_general(q, kmat, (((1,), (1,)), ((), ())), preferred_element_type=F32)
        rowh = lax.broadcasted_iota(jnp.int32, (nrow, r), 0) % n_heads
        colh = lax.broadcasted_iota(jnp.int32, (nrow, r), 1) % n_heads
        s = jnp.where(rowh == colh, s + bias, NEG)
        m_prev = m_sc[...]
        m_new = jnp.maximum(m_prev, jnp.max(s, axis=-1, keepdims=True))
        alpha = jnp.exp(m_prev - m_new)
        pr = jnp.where(rowh == colh, jnp.exp(s - m_new), 0.0)
        l_sc[...] = alpha * l_sc[...] + jnp.sum(pr, axis=-1, keepdims=True)
        acc_sc[...] = alpha * acc_sc[...] + jnp.dot(pr.astype(BF16), vmat, preferred_element_type=F32)
        m_sc[...] = m_new

    for u in range(n_sub):
        kp = k_refs[u][...]
        vp = v_refs[u][...]
        kmat = kp.reshape(kp.shape[0] * kp.shape[1], kp.shape[2]).astype(BF16)
        vmat = vp.reshape(vp.shape[0] * vp.shape[1], vp.shape[2]).astype(BF16)
        if u == n_sub - 1:
            @pl.when(s_idx == last)
            def _():
                update(kmat, vmat, bias_last_ref[...])

            @pl.when(s_idx != last)
            def _():
                update(kmat, vmat, far_ref[...])
        else:
            update(kmat, vmat, far_ref[...])

    @pl.when(s_idx == last)
    def _():
        update(kn_ref[0], vn_ref[0], own_ref[...])
        lv = lam_ref[...]
        lam = (jnp.exp(jnp.sum(lv[0:1] * lv[1:2], axis=-1, keepdims=True))
               - jnp.exp(jnp.sum(lv[2:3] * lv[3:4], axis=-1, keepdims=True)) + lam_init)
        on = acc_sc[...] / l_sc[...]
        o = on[:n_heads] - lam * on[n_heads:]
        ms = jnp.mean(o * o, axis=-1, keepdims=True)
        o_ref[0] = o * lax.rsqrt(ms + EPS) * g_ref[...] * (1.0 - lam_init)


def decode_attention(layer, q, k_new, v_new, cache_k, cache_v, page_table, rel_bias, lam_vecs, diff_g,
                     lam_init):
    nb, n_pages = page_table.shape
    _, _, page, n_heads, dv = cache_v.shape
    dh = dv // 2
    n_sub = 4 if n_pages % 4 == 0 else 1
    n_steps = n_pages // n_sub
    rows = page * n_heads
    qh = q.reshape(nb, n_heads, 2 * dh)
    half = (np.arange(2 * dh) // dh)[None, :] == np.arange(2)[:, None]
    qall = (qh[:, None, :, :] * jnp.asarray(half, F32)[None, :, None, :]).reshape(nb, 2 * n_heads, 2 * dh)
    qall = qall.astype(BF16)
    n_buckets = rel_bias.shape[0]
    table = _bucket_table(page + 1, n_buckets)
    rb = rel_bias.astype(F32)
    head_of_row = np.arange(2 * n_heads) % n_heads
    dist_last = page - (np.arange(rows) // n_heads)
    bias_last = rb[table[dist_last][None, :], head_of_row[:, None]]
    assert page + 1 >= MAX_DISTANCE
    far = rb[n_buckets - 1][head_of_row][:, None]
    own = rb[0][head_of_row][:, None]
    kn = k_new.reshape(nb, n_heads, 2 * dh).astype(BF16)
    vn = v_new.reshape(nb, n_heads, dv).astype(BF16)
    pt_flat = page_table.reshape(-1).astype(jnp.int32)

    def page_spec(u):
        return pl.BlockSpec((None, None, page, n_heads, dv),
                            lambda b, s, pt, u=u: (layer, pt[b * n_pages + s * n_sub + u], 0, 0, 0))

    const2 = lambda b, s, pt: (0, 0)
    grid_spec = pltpu.PrefetchScalarGridSpec(
        num_scalar_prefetch=1,
        grid=(nb, n_steps),
        in_specs=[pl.BlockSpec((1, 2 * n_heads, 2 * dh), lambda b, s, pt: (b, 0, 0))]
        + [page_spec(u) for u in range(n_sub)] + [page_spec(u) for u in range(n_sub)]
        + [pl.BlockSpec((2 * n_heads, rows), const2),
           pl.BlockSpec((2 * n_heads, 1), const2),
           pl.BlockSpec((2 * n_heads, 1), const2),
           pl.BlockSpec((1, n_heads, 2 * dh), lambda b, s, pt: (b, 0, 0)),
           pl.BlockSpec((1, n_heads, dv), lambda b, s, pt: (b, 0, 0)),
           pl.BlockSpec((4, dh), const2),
           pl.BlockSpec((1, dv), const2)],
        out_specs=pl.BlockSpec((1, n_heads, dv), lambda b, s, pt: (b, 0, 0)),
        scratch_shapes=[pltpu.VMEM((2 * n_heads, 1), F32), pltpu.VMEM((2 * n_heads, 1), F32),
                        pltpu.VMEM((2 * n_heads, dv), F32)],
    )
    kern = functools.partial(_decode_attn_kernel, n_sub=n_sub, n_heads=n_heads, lam_init=lam_init)
    return pl.pallas_call(
        kern, grid_spec=grid_spec,
        out_shape=jax.ShapeDtypeStruct((nb, n_heads, dv), F32),
        compiler_params=_cparams(("parallel", "arbitrary")),
        name="diff_attn_decode",
    )(pt_flat, qall, *([cache_k] * n_sub), *([cache_v] * n_sub), bias_last, far, own, kn, vn,
      lam_vecs, diff_g.reshape(1, dv).astype(F32))


def _gla_step_kernel(s_ref, a_ref, k_ref, q_ref, v_ref, r_ref, gn_ref, s_out_ref, y_ref):
    s_new = jnp.exp(a_ref[0, 0]) * s_ref[...] + k_ref[0, 0] * v_ref[0, 0]
    s_out_ref[0, 0] = s_new
    o = jnp.sum(q_ref[0, 0] * s_new, axis=0, keepdims=True)
    ms = jnp.mean(o * o, axis=-1, keepdims=True)
    rr = r_ref[0, 0]
    y_ref[0, 0] = o * lax.rsqrt(ms + EPS) * gn_ref[...] * (rr * jax.nn.sigmoid(rr))


def decode_gla(layer, state, log_a, k, q, v, r, gla_g):
    _, nb, n_heads, dk, dv = state.shape
    col = lambda t: t.reshape(nb, n_heads, dk, 1)
    rowv = lambda t: t.reshape(nb, n_heads, 1, dv)
    cspec = pl.BlockSpec((1, 1, dk, 1), lambda b, h: (b, h, 0, 0))
    rspec = pl.BlockSpec((1, 1, 1, dv), lambda b, h: (b, h, 0, 0))
    return pl.pallas_call(
        _gla_step_kernel,
        grid=(nb, n_heads),
        in_specs=[pl.BlockSpec((None, None, None, dk, dv), lambda b, h: (layer, b, h, 0, 0)),
                  cspec, cspec, cspec, rspec, rspec,
                  pl.BlockSpec((1, dv), lambda b, h: (0, 0))],
        out_specs=[pl.BlockSpec((1, 1, dk, dv), lambda b, h: (b, h, 0, 0)), rspec],
        out_shape=[jax.ShapeDtypeStruct((nb, n_heads, dk, dv), F32),
                   jax.ShapeDtypeStruct((nb, n_heads, 1, dv), F32)],
        compiler_params=_cparams(("parallel", "parallel")),
        name="gla_decode",
    )(state, col(log_a), col(k), col(q), rowv(v), rowv(r), gla_g.reshape(1, dv).astype(F32))


def _layer_weights(i, w_in, sizes, others):
    offs = np.concatenate([[0], np.cumsum(sizes)])
    wi = w_in[i]
    seg = lambda a, b: wi[:, offs[a]:offs[b]].astype(BF16)
    w = {"q": seg(0, 1), "k": seg(1, 2), "v": seg(2, 3),
         "rest": jnp.concatenate([wi[:, offs[3]:offs[7]], wi[:, offs[8]:offs[10]]], axis=1).astype(BF16),
         "glow": seg(7, 8)}
    for name, arr in others.items():
        w[name] = arr[i].astype(BF16)
    return w


def _dense_front(x2, lw, g_mix, q_g, k_g, gate_w2, gate_b, dh, tm):
    h = rmsnorm_rows(x2, g_mix)
    d_qk = lw["q"].shape[1]
    qn = functools.partial(_ep_groupnorm, group=dh, scale=dh ** -0.5)
    kn = functools.partial(_ep_groupnorm, group=dh, scale=1.0)
    gq = q_g.reshape(1, dh).astype(F32)
    gk = k_g.reshape(1, dh).astype(F32)
    (q,) = fused_matmul([h], [(0, lw["q"])], [(gq, "full", 0)], [(d_qk, BF16)], qn, tm=tm, name="in_q")
    k32, k16 = fused_matmul([h], [(0, lw["k"])], [(gk, "full", 0)], [(d_qk, F32), (d_qk, BF16)], kn,
                            tm=tm, name="in_k")
    d_v = lw["v"].shape[1]
    v32, v16 = fused_matmul([h], [(0, lw["v"])], [], [(d_v, F32), (d_v, BF16)], _ep_cast, tm=tm, name="in_v")
    (rest,) = fused_matmul([h], [(0, lw["rest"])], [], [(lw["rest"].shape[1], BF16)], _ep_cast, tm=tm,
                           name="in_rest")
    n_gate = gate_w2.shape[1]
    (log_a,) = fused_matmul([h], [(0, lw["glow"])],
                            [(gate_w2.astype(F32), "full", 0), (gate_b.reshape(1, n_gate).astype(F32), "full", 0)],
                            [(n_gate, F32)], _ep_gate, tm=tm, name="in_gate")
    return q, k32, k16, v32, v16, rest, log_a


def _dense_back(x2, ya, yb, rest, off_ga, off_gb, p2, lw, g_ffn, g_ple, tm):
    d = x2.shape[1]
    tn = _tile(d, 1024)
    tn_m = _tile(d, 512)
    assert off_ga % tn_m == 0 and off_gb % tn_m == 0
    (merged,) = fused_matmul([ya, yb], [(0, lw["w_pa"]), (1, lw["w_pb"])],
                             [(rest, "tile", off_ga // tn_m), (rest, "tile", off_gb // tn_m)],
                             [(d, BF16)], _ep_merge, tm=tm, tn=tn_m, name="merge")
    (x2,) = fused_matmul([merged], [(0, lw["w_o"])], [(x2, "tile", 0)], [(d, F32)], _ep_residual,
                         tm=tm, tn=tn, name="out_proj")
    h = rmsnorm_rows(x2, g_ffn)
    d_ff = lw["w_gate"].shape[1]
    (gu,) = fused_matmul([h], [(0, lw["w_gate"]), (0, lw["w_up"])], [], [(d_ff, BF16)], _ep_swiglu,
                         tm=tm, tn=512, name="ffn_up")
    (x2,) = fused_matmul([gu], [(0, lw["w_down"])], [(x2, "tile", 0)], [(d, F32)], _ep_residual,
                         tm=min(tm, 512), tn=512, name="ffn_down")
    h = rmsnorm_rows(x2, g_ple)
    (x2,) = fused_matmul([h, p2], [(0, lw["w_ple_gate"]), (1, lw["w_ple_proj"])], [(x2, "tile", 0)],
                         [(d, F32)], _ep_ple, tm=tm, tn=tn, name="ple")
    return x2


def kernel(x_prompt, x_sample, cache_k, cache_v, state_gla, page_table, p_prompt, p_sample, rel_bias, norm_mix_g, w_in, gla_gate_w2, gla_gate_b, q_norm_g, k_norm_g, lambda_q1, lambda_k1, lambda_q2, lambda_k2, diff_norm_g, gla_norm_g, w_pa, w_pb, w_o, norm_ffn_g, w_gate, w_up, w_down, norm_ple_g, w_ple_gate, w_ple_proj):
    depth = w_in.shape[0]
    bp, lp, d = x_prompt.shape
    bs, ls, _ = x_sample.shape
    assert ls == 1
    _, _, page, h_a, dv_a = cache_v.shape
    dh_a = dv_a // 2
    d_a = h_a * dv_a
    _, _, h_b, dk_b, dv_b = state_gla.shape
    d_b = h_b * dv_b
    rank = gla_gate_w2.shape[1]
    sizes = [d_a, d_a, d_a, h_b * dk_b, h_b * dk_b, d_b, d_b, rank, d, d]
    off_qb, off_kb = 0, h_b * dk_b
    off_vb = 2 * h_b * dk_b
    off_r = off_vb + d_b
    off_ga = off_r + d_b
    off_gb = off_ga + d
    others = {"w_pa": w_pa, "w_pb": w_pb, "w_o": w_o, "w_gate": w_gate, "w_up": w_up, "w_down": w_down,
              "w_ple_gate": w_ple_gate, "w_ple_proj": w_ple_proj}

    mp = bp * lp
    ms = 16
    xp = x_prompt.reshape(mp, d)
    xs = jnp.zeros((ms, d), F32).at[:bs].set(x_sample.reshape(bs, d))

    kp_l, vp_l, sp_l, ks_l, vs_l, ss_l = [], [], [], [], [], []
    for i in range(depth):
        lam_init = 0.8 - 0.6 * math.exp(-0.3 * i)
        lw = _layer_weights(i, w_in, sizes, others)
        lam_vecs = jnp.stack([lambda_q1[i], lambda_k1[i], lambda_q2[i], lambda_k2[i]]).astype(F32)

        q, k32, k16, v32, v16, rest, log_a = _dense_front(
            xp, lw, norm_mix_g[i], q_norm_g[i], k_norm_g[i], gla_gate_w2[i], gla_gate_b[i], dh_a, 1024)
        to3 = lambda t: t.reshape(bp, lp, t.shape[1])
        ya = prompt_attention(to3(q), to3(k16), to3(v16), rel_bias, lam_vecs, diff_norm_g[i], lam_init, h_a)
        yb, st = prompt_gla(to3(rest), to3(log_a), gla_norm_g[i], h_b, dk_b, dv_b, off_qb, off_kb, off_vb, off_r)
        xp = _dense_back(xp, ya.reshape(mp, d_a), yb.reshape(mp, d_b), rest, off_ga, off_gb,
                         p_prompt[i].reshape(mp, -1), lw, norm_ffn_g[i], norm_ple_g[i], 1024)
        kp_l.append(k32.reshape(bp, lp, h_a, dv_a))
        vp_l.append(v32.reshape(bp, lp, h_a, dv_a))
        sp_l.append(jnp.swapaxes(st, 2, 3))

        q, k32, k16, v32, v16, rest, log_a = _dense_front(
            xs, lw, norm_mix_g[i], q_norm_g[i], k_norm_g[i], gla_gate_w2[i], gla_gate_b[i], dh_a, ms)
        ya = decode_attention(i, q[:bs].astype(F32), k32[:bs], v32[:bs], cache_k, cache_v, page_table,
                              rel_bias, lam_vecs, diff_norm_g[i], lam_init)
        rest32 = rest[:bs].astype(F32)
        hk = lambda t: t.reshape(bs, h_b, -1)
        s_new, yb = decode_gla(i, state_gla, hk(log_a[:bs]),
                               hk(rest32[:, off_kb:off_kb + h_b * dk_b]),
                               hk(rest32[:, off_qb:off_qb + h_b * dk_b]) * (dk_b ** -0.5),
                               hk(rest32[:, off_vb:off_vb + d_b]), hk(rest32[:, off_r:off_r + d_b]),
                               gla_norm_g[i])
        pad = lambda t: jnp.zeros((ms, t.shape[1]), BF16).at[:bs].set(t.astype(BF16))
        ps = jnp.zeros((ms, p_sample.shape[-1]), F32).at[:bs].set(p_sample[i].reshape(bs, -1))
        xs = _dense_back(xs, pad(ya.reshape(bs, d_a)), pad(yb.reshape(bs, d_b)), rest, off_ga, off_gb,
                         ps, lw, norm_ffn_g[i], norm_ple_g[i], ms)
        ks_l.append(k32[:bs].reshape(bs, 1, h_a, dv_a))
        vs_l.append(v32[:bs].reshape(bs, 1, h_a, dv_a))
        ss_l.append(s_new)

    return (xp.reshape(bp, lp, d), xs[:bs].reshape(bs, 1, d),
            jnp.stack(kp_l), jnp.stack(vp_l), jnp.stack(sp_l),
            jnp.stack(ks_l), jnp.stack(vs_l), jnp.stack(ss_l))
```

```python
import functools
import math

import numpy as np
import jax
import jax.numpy as jnp
from jax import lax
from jax.experimental import pallas as pl
from jax.experimental.pallas import tpu as pltpu

EPS = 1e-6
GATE_TAU = 16.0
GLA_CHUNK = 64
MAX_EXACT = 16
MAX_DISTANCE = 128
LOG2E = math.log2(math.e)
NEG = -1e30
VMEM_LIMIT = 48 * 1024 * 1024
ATTN_TILE = 512
GLA_BLOCK = 512
ROW_TILE = 1024
SAMPLE_ROWS = 16
DECODE_PAGES = 4

BF16 = jnp.bfloat16
F32 = jnp.float32


def _cparams(sem):
    return pltpu.CompilerParams(dimension_semantics=sem, vmem_limit_bytes=VMEM_LIMIT)


def _tile(n, pref):
    if n <= pref:
        return n
    t = pref
    while n % t:
        t //= 2
    return t


def _rmsnorm_kernel(x_ref, g_ref, o_ref):
    x = x_ref[...]
    ms = jnp.mean(x * x, axis=-1, keepdims=True)
    o_ref[...] = (x * lax.rsqrt(ms + EPS) * g_ref[...]).astype(o_ref.dtype)


def rmsnorm_rows(x, g):
    m, d = x.shape
    tm = _tile(m, 512)
    return pl.pallas_call(
        _rmsnorm_kernel,
        grid=(m // tm,),
        in_specs=[pl.BlockSpec((tm, d), lambda i: (i, 0)),
                  pl.BlockSpec((1, d), lambda i: (0, 0))],
        out_specs=pl.BlockSpec((tm, d), lambda i: (i, 0)),
        out_shape=jax.ShapeDtypeStruct((m, d), BF16),
        compiler_params=_cparams(("parallel",)),
        name="rmsnorm",
    )(x, g.reshape(1, d).astype(F32))


class _Out:
    def __init__(self, ref, transposed):
        self.ref = ref
        self.transposed = transposed

    def put(self, val, c0=None, c1=None):
        if self.transposed:
            val = val.T.astype(self.ref.dtype)
            if c0 is None:
                self.ref[...] = val
            else:
                self.ref[c0:c1, :] = val
        else:
            val = val.astype(self.ref.dtype)
            if c0 is None:
                self.ref[...] = val
            else:
                self.ref[:, c0:c1] = val


def _mm_kernel(*refs, pair_a, n_a, n_extra, out_t, epilogue):
    n_w = len(pair_a)
    a_refs = refs[:n_a]
    w_refs = refs[n_a:n_a + n_w]
    extras = refs[n_a + n_w:n_a + n_w + n_extra]
    out_refs = refs[len(refs) - len(out_t):]
    a_vals = [a[...].astype(BF16) for a in a_refs]
    accs = [jnp.dot(a_vals[ai], w[...], preferred_element_type=F32)
            for ai, w in zip(pair_a, w_refs)]
    epilogue(accs, extras, [_Out(r, t) for r, t in zip(out_refs, out_t)])


def fused_matmul(a_list, pairs, extras, outs, epilogue, *, tm, tn=1024, stacked=None, name="mm"):
    m = a_list[0].shape[0]
    n = pairs[0][1].shape[1]
    tm = _tile(m, tm)
    tn = _tile(n, tn)
    grid = (m // tm, n // tn)
    in_specs, args = [], []
    for a in a_list:
        in_specs.append(pl.BlockSpec((tm, a.shape[1]), lambda i, j: (i, 0)))
        args.append(a)
    for _, w in pairs:
        in_specs.append(pl.BlockSpec((w.shape[0], tn), lambda i, j: (0, j)))
        args.append(w)
    for arr, kind, off in extras:
        if kind == "tile":
            in_specs.append(pl.BlockSpec((tm, tn), lambda i, j, off=off: (i, off + j)))
        else:
            in_specs.append(pl.BlockSpec(arr.shape, lambda i, j, nd=arr.ndim: (0,) * nd))
        args.append(arr)
    out_specs, out_shape, aliases = [], [], {}
    for oi, (width, dt, kind) in enumerate(outs):
        if stacked is not None and stacked[0] == oi:
            _, layer, buf = stacked
            assert kind == "n" and buf.shape[1:] == (m, n) and buf.dtype == dt
            out_specs.append(pl.BlockSpec((None, tm, tn), lambda i, j, layer=layer: (layer, i, j)))
            out_shape.append(jax.ShapeDtypeStruct(buf.shape, dt))
            in_specs.append(pl.BlockSpec(memory_space=pl.ANY))
            aliases[len(args)] = oi
            args.append(buf)
        elif kind == "n":
            out_specs.append(pl.BlockSpec((tm, tn), lambda i, j: (i, j)))
            out_shape.append(jax.ShapeDtypeStruct((m, n), dt))
        elif kind == "t":
            out_specs.append(pl.BlockSpec((tn, tm), lambda i, j: (j, i)))
            out_shape.append(jax.ShapeDtypeStruct((n, m), dt))
        else:
            assert grid[1] == 1
            out_specs.append(pl.BlockSpec((tm, width), lambda i, j: (i, 0)))
            out_shape.append(jax.ShapeDtypeStruct((m, width), dt))
    kern = functools.partial(_mm_kernel, pair_a=tuple(p[0] for p in pairs), n_a=len(a_list),
                             n_extra=len(extras), out_t=tuple(o[2] == "t" for o in outs),
                             epilogue=epilogue)
    return pl.pallas_call(
        kern, grid=grid, in_specs=in_specs, out_specs=out_specs, out_shape=out_shape,
        input_output_aliases=aliases,
        compiler_params=_cparams(("parallel", "arbitrary")), name=name,
    )(*args)


def _ep_cast(accs, extras, outs):
    for o in outs:
        o.put(accs[0])


def _ep_groupnorm(accs, extras, outs, *, group, scale):
    acc = accs[0]
    g = extras[0][...] * scale
    for s in range(acc.shape[1] // group):
        z = acc[:, s * group:(s + 1) * group]
        ms = jnp.mean(z * z, axis=-1, keepdims=True)
        y = z * lax.rsqrt(ms + EPS) * g
        for o in outs:
            o.put(y, s * group, (s + 1) * group)


def _ep_gate(accs, extras, outs):
    y = jnp.dot(accs[0], extras[0][...], preferred_element_type=F32,
                precision=lax.Precision.HIGHEST) + extras[1][...]
    ls = jnp.minimum(y, 0.0) - jnp.log1p(jnp.exp(-jnp.abs(y)))
    outs[0].put(ls * (1.0 / GATE_TAU))


def _ep_merge(accs, extras, outs):
    ga = extras[0][...].astype(F32)
    gb = extras[1][...].astype(F32)
    outs[0].put(jax.nn.sigmoid(ga) * accs[0] + jax.nn.sigmoid(gb) * accs[1])


def _ep_residual(accs, extras, outs):
    outs[0].put(extras[0][...] + accs[0])


def _ep_swiglu(accs, extras, outs):
    g = accs[0]
    outs[0].put(g * jax.nn.sigmoid(g) * accs[1])


def _ep_ple(accs, extras, outs):
    outs[0].put(extras[0][...] + jax.nn.sigmoid(accs[0]) * accs[1])


def _bucket_table(n, n_buckets):
    d = np.arange(n)
    nf = np.maximum(d, 1).astype(np.float64)
    large = MAX_EXACT + (np.log(nf / MAX_EXACT) / math.log(MAX_DISTANCE / MAX_EXACT)
                         * (n_buckets - MAX_EXACT)).astype(np.int64)
    large = np.minimum(large, n_buckets - 1)
    return np.where(d < MAX_EXACT, d, large).astype(np.int32)


def _toeplitz(w, t):
    n = 2 * t
    big = jnp.tile(w, (1, t))[:, :t * (n - 1)].reshape(w.shape[0], t, n - 1)
    return big[:, :, :t]


def _prompt_bias_tiles(rel_bias, t):
    n_buckets = rel_bias.shape[0]
    assert t + 1 >= MAX_DISTANCE
    table = _bucket_table(2 * t, n_buckets)
    rb = rel_bias.astype(F32)
    f = ((rb[table] - rb[n_buckets - 1][None, :]) * LOG2E).T
    neg = jnp.full((f.shape[0], t), NEG, F32)
    diag = _toeplitz(jnp.concatenate([f[:, :t], neg], axis=1), t)
    sub = _toeplitz(jnp.concatenate([f[:, t:], f[:, :t]], axis=1), t)
    return jnp.stack([diag, sub], axis=1)


def _decode_bias(rel_bias, page, n_heads):
    n_buckets = rel_bias.shape[0]
    assert page + 1 >= MAX_DISTANCE
    table = _bucket_table(page + 1, n_buckets)
    rel = (rel_bias.astype(F32) - rel_bias.astype(F32)[n_buckets - 1][None, :]) * LOG2E
    head_of_row = np.arange(2 * n_heads) % n_heads
    by_dist = rel[table][:, head_of_row].T
    last = jnp.repeat(by_dist[:, page - np.arange(page)], n_heads, axis=1)
    own = by_dist[:, 0:1]
    return last, own


def _attn_kernel(q_ref, k_ref, v_ref, bias_ref, lam_ref, g_ref, o_ref, m_sc, l_sc, acc_sc,
                 *, lam_init, dh, t):
    qi = pl.program_id(2)
    m_sc[...] = jnp.full(m_sc.shape, NEG, F32)
    l_sc[...] = jnp.zeros(l_sc.shape, F32)
    acc_sc[...] = jnp.zeros(acc_sc.shape, F32)

    def tile(j, bias):
        ks = pl.ds(pl.multiple_of(j * t, t), t)
        vt = v_ref[:, ks]
        for c in range(2):
            kc = k_ref[0, ks, c * dh:(c + 1) * dh]
            qc = q_ref[c * dh:(c + 1) * dh, :]
            s = jnp.dot(kc, qc, preferred_element_type=F32)
            if bias is not None:
                s = s + bias
            m_prev = m_sc[c]
            m_new = jnp.maximum(m_prev, jnp.max(s, axis=0, keepdims=True))
            alpha = jnp.exp2(m_prev - m_new)
            pr = jnp.exp2(s - m_new)
            l_sc[c] = alpha * l_sc[c] + jnp.sum(pr, axis=0, keepdims=True)
            acc_sc[c] = alpha * acc_sc[c] + jnp.dot(vt, pr.astype(BF16), preferred_element_type=F32)
            m_sc[c] = m_new

    def far(j, carry):
        tile(j, None)
        return carry

    lax.fori_loop(0, qi - 1, far, 0)

    @pl.when(qi >= 1)
    def _():
        tile(qi - 1, bias_ref[0, 1])

    tile(qi, bias_ref[0, 0])

    lv = lam_ref[...]
    lam = (jnp.exp(jnp.sum(lv[0:1] * lv[1:2], axis=-1, keepdims=True))
           - jnp.exp(jnp.sum(lv[2:3] * lv[3:4], axis=-1, keepdims=True)) + lam_init)
    r0 = 1.0 / l_sc[0]
    r1 = lam / l_sc[1]
    ot = acc_sc[0] * r0 - acc_sc[1] * r1
    ms = jnp.mean(ot * ot, axis=0, keepdims=True)
    yt = ot * lax.rsqrt(ms + EPS) * (g_ref[...] * (1.0 - lam_init))
    o_ref[0] = yt.T.astype(o_ref.dtype)


def prompt_attention(qt, k, vt, bias, lam_vecs, diff_g, lam_init, n_heads, b, l):
    dtot = k.shape[2]
    dv = dtot // n_heads
    dh = dv // 2
    t = bias.shape[-1]
    nq = l // t
    kern = functools.partial(_attn_kernel, lam_init=lam_init, dh=dh, t=t)
    return pl.pallas_call(
        kern,
        grid=(b, n_heads, nq),
        in_specs=[
            pl.BlockSpec((dv, t), lambda bi, h, qi: (h, bi * nq + qi)),
            pl.BlockSpec((1, l, dv), lambda bi, h, qi: (bi, 0, h)),
            pl.BlockSpec((dv, l), lambda bi, h, qi: (h, bi)),
            pl.BlockSpec((1, 2, t, t), lambda bi, h, qi: (h, 0, 0, 0)),
            pl.BlockSpec((4, dh), lambda bi, h, qi: (0, 0)),
            pl.BlockSpec((dv, 1), lambda bi, h, qi: (0, 0)),
        ],
        out_specs=pl.BlockSpec((1, t, dv), lambda bi, h, qi: (bi, qi, h)),
        out_shape=jax.ShapeDtypeStruct((b, l, dtot), BF16),
        scratch_shapes=[pltpu.VMEM((2, 1, t), F32), pltpu.VMEM((2, 1, t), F32),
                        pltpu.VMEM((2, dv, t), F32)],
        compiler_params=_cparams(("parallel", "parallel", "arbitrary")),
        name="diff_attn_prompt",
    )(qt, k, vt, bias, lam_vecs, diff_g.reshape(dv, 1).astype(F32))


def _gla_kernel(q_ref, k_ref, v_ref, r_ref, g_ref, gn_ref, y_ref, s_ref, st_sc, *, chunk, nchunk, qscale):
    li = pl.program_id(1)
    nb = q_ref.shape[0]

    @pl.when(li == 0)
    def _():
        st_sc[...] = jnp.zeros(st_sc.shape, F32)

    lb = nchunk * chunk
    row = lax.broadcasted_iota(jnp.int32, (chunk, chunk), 0)
    col = lax.broadcasted_iota(jnp.int32, (chunk, chunk), 1)
    tril = (row >= col).astype(BF16)
    brow = lax.broadcasted_iota(jnp.int32, (lb, lb), 0)
    bcol = lax.broadcasted_iota(jnp.int32, (lb, lb), 1)
    block_causal = (brow >= bcol) & (brow // chunk == bcol // chunk)
    gn = gn_ref[...]
    nt = (((1,), (1,)), ((), ()))
    tn = (((0,), (0,)), ((), ()))

    pre = []
    for b in range(nb):
        g = g_ref[b]
        g_hi = g.astype(BF16)
        r1 = g - g_hi.astype(F32)
        g_mid = r1.astype(BF16)
        g_lo = (r1 - g_mid.astype(F32)).astype(BF16)
        cums, tots = [], []
        for c in range(nchunk):
            sl = slice(c * chunk, (c + 1) * chunk)
            bc = (jnp.dot(tril, g_hi[sl], preferred_element_type=F32)
                  + jnp.dot(tril, g_mid[sl], preferred_element_type=F32)
                  + jnp.dot(tril, g_lo[sl], preferred_element_type=F32))
            cums.append(bc)
            tots.append(jnp.broadcast_to(bc[chunk - 1:chunk, :], bc.shape))
        bcum = jnp.concatenate(cums, axis=0)
        btot = jnp.concatenate(tots, axis=0)
        q = q_ref[b].astype(F32) * qscale
        k = k_ref[b].astype(F32)
        v = v_ref[b]
        qt = (q * jnp.exp(bcum)).astype(BF16)
        kt = (k * jnp.exp(-bcum)).astype(BF16)
        k2 = (k * jnp.exp(btot - bcum)).astype(BF16)
        a = lax.dot_general(qt, kt, nt, preferred_element_type=F32)
        a = jnp.where(block_causal, a, 0.0)
        o_intra = jnp.dot(a.astype(BF16), v, preferred_element_type=F32)
        pre.append((qt, k2, v, o_intra, jnp.exp(btot)))

    states = [st_sc[b] for b in range(nb)]
    outs = [[] for _ in range(nb)]
    for c in range(nchunk):
        sl = slice(c * chunk, (c + 1) * chunk)
        for b in range(nb):
            qt, k2, v, o_intra, decay = pre[b]
            st = states[b]
            outs[b].append(o_intra[sl] + lax.dot_general(qt[sl], st.astype(BF16), nt,
                                                         preferred_element_type=F32))
            states[b] = st * decay[c * chunk:c * chunk + 1, :] + lax.dot_general(
                v[sl], k2[sl], tn, preferred_element_type=F32)

    for b in range(nb):
        st_sc[b] = states[b]
        o = jnp.concatenate(outs[b], axis=0)
        ms = jnp.mean(o * o, axis=-1, keepdims=True)
        rr = r_ref[b].astype(F32)
        y = o * lax.rsqrt(ms + EPS) * gn * (rr * jax.nn.sigmoid(rr))
        y_ref[b] = y.astype(y_ref.dtype)

    @pl.when(li == pl.num_programs(1) - 1)
    def _():
        s_ref[:, 0] = st_sc[...]


def prompt_gla(z, log_a, gla_g, n_heads, dk, dv, off_q, off_k, off_v, off_r):
    b, l, _ = z.shape
    chunk = math.gcd(l, GLA_CHUNK)
    lb = _tile(l, GLA_BLOCK)
    nchunk = lb // chunk
    kern = functools.partial(_gla_kernel, chunk=chunk, nchunk=nchunk, qscale=dk ** -0.5)
    oq, ok, ov, orr = off_q // dk, off_k // dk, off_v // dv, off_r // dv
    return pl.pallas_call(
        kern,
        grid=(n_heads, l // lb),
        in_specs=[
            pl.BlockSpec((b, lb, dk), lambda h, i: (0, i, oq + h)),
            pl.BlockSpec((b, lb, dk), lambda h, i: (0, i, ok + h)),
            pl.BlockSpec((b, lb, dv), lambda h, i: (0, i, ov + h)),
            pl.BlockSpec((b, lb, dv), lambda h, i: (0, i, orr + h)),
            pl.BlockSpec((b, lb, dk), lambda h, i: (0, i, h)),
            pl.BlockSpec((1, dv), lambda h, i: (0, 0)),
        ],
        out_specs=[
            pl.BlockSpec((b, lb, dv), lambda h, i: (0, i, h)),
            pl.BlockSpec((b, 1, dv, dk), lambda h, i: (0, h, 0, 0)),
        ],
        out_shape=[jax.ShapeDtypeStruct((b, l, n_heads * dv), BF16),
                   jax.ShapeDtypeStruct((b, n_heads, dv, dk), F32)],
        scratch_shapes=[pltpu.VMEM((b, dv, dk), F32)],
        compiler_params=_cparams(("parallel", "arbitrary")),
        name="gla_prompt",
    )(z, z, z, z, log_a, gla_g.reshape(1, dv).astype(F32))


def _decode_attn_kernel(pt_ref, q_ref, *refs, n_sub, n_heads, lam_init):
    k_refs = refs[:n_sub]
    v_refs = refs[n_sub:2 * n_sub]
    (bias_last_ref, own_ref, kn_ref, vn_ref, lam_ref, g_ref, o_ref, m_sc, l_sc, acc_sc) = refs[2 * n_sub:]
    s_idx = pl.program_id(1)
    last = pl.num_programs(1) - 1

    @pl.when(s_idx == 0)
    def _():
        m_sc[...] = jnp.full(m_sc.shape, NEG, F32)
        l_sc[...] = jnp.zeros(l_sc.shape, F32)
        acc_sc[...] = jnp.zeros(acc_sc.shape, F32)

    q = q_ref[0]
    nrow = q.shape[0]

    def update(kmats, vmats, biases):
        parts = []
        for kmat, bias in zip(kmats, biases):
            sp = lax.dot_general(q, kmat, (((1,), (1,)), ((), ())), preferred_element_type=F32)
            parts.append(sp if bias is None else sp + bias)
        s = parts[0] if len(parts) == 1 else jnp.concatenate(parts, axis=1)
        r = s.shape[1]
        rowh = lax.broadcasted_iota(jnp.int32, (nrow, r), 0) % n_heads
        colh = lax.broadcasted_iota(jnp.int32, (nrow, r), 1) % n_heads
        s = jnp.where(rowh == colh, s, NEG)
        m_prev = m_sc[...]
        m_new = jnp.maximum(m_prev, jnp.max(s, axis=-1, keepdims=True))
        alpha = jnp.exp2(m_prev - m_new)
        pr = jnp.exp2(s - m_new).astype(BF16)
        l_sc[...] = alpha * l_sc[...] + jnp.sum(pr.astype(F32), axis=-1, keepdims=True)
        pv = None
        off = 0
        for vmat in vmats:
            d = jnp.dot(pr[:, off:off + vmat.shape[0]], vmat, preferred_element_type=F32)
            pv = d if pv is None else pv + d
            off += vmat.shape[0]
        acc_sc[...] = alpha * acc_sc[...] + pv
        m_sc[...] = m_new

    kmats, vmats = [], []
    for u in range(n_sub):
        kp = k_refs[u][...]
        vp = v_refs[u][...]
        kmats.append(kp.reshape(kp.shape[0] * kp.shape[1], kp.shape[2]).astype(BF16))
        vmats.append(vp.reshape(vp.shape[0] * vp.shape[1], vp.shape[2]).astype(BF16))
    is_last = (s_idx == last).astype(F32)
    update(kmats, vmats, [None] * (n_sub - 1) + [bias_last_ref[...] * is_last])

    @pl.when(s_idx == last)
    def _():
        update([kn_ref[0]], [vn_ref[0]], [own_ref[...]])
        lv = lam_ref[...]
        lam = (jnp.exp(jnp.sum(lv[0:1] * lv[1:2], axis=-1, keepdims=True))
               - jnp.exp(jnp.sum(lv[2:3] * lv[3:4], axis=-1, keepdims=True)) + lam_init)
        on = acc_sc[...] / l_sc[...]
        o = on[:n_heads] - lam * on[n_heads:]
        ms = jnp.mean(o * o, axis=-1, keepdims=True)
        o_ref[0] = o * lax.rsqrt(ms + EPS) * g_ref[...] * (1.0 - lam_init)


def decode_attention(layer, q, k_new, v_new, cache_k, cache_v, page_table, bias_last, bias_own, lam_vecs,
                     diff_g, lam_init):
    nb, n_pages = page_table.shape
    _, _, page, n_heads, dv = cache_v.shape
    dh = dv // 2
    n_sub = DECODE_PAGES if n_pages % DECODE_PAGES == 0 else 1
    n_steps = n_pages // n_sub
    rows = page * n_heads
    qh = q.reshape(nb, n_heads, 2 * dh)
    half = (np.arange(2 * dh) // dh)[None, :] == np.arange(2)[:, None]
    qall = (qh[:, None, :, :] * jnp.asarray(half, F32)[None, :, None, :]).reshape(nb, 2 * n_heads, 2 * dh)
    qall = qall.astype(BF16)
    kn = k_new.reshape(nb, n_heads, 2 * dh).astype(BF16)
    vn = v_new.reshape(nb, n_heads, dv).astype(BF16)
    pt_flat = page_table.reshape(-1).astype(jnp.int32)

    def page_spec(u):
        return pl.BlockSpec((None, None, page, n_heads, dv),
                            lambda b, s, pt, u=u: (layer, pt[b * n_pages + s * n_sub + u], 0, 0, 0))

    const2 = lambda b, s, pt: (0, 0)
    grid_spec = pltpu.PrefetchScalarGridSpec(
        num_scalar_prefetch=1,
        grid=(nb, n_steps),
        in_specs=[pl.BlockSpec((1, 2 * n_heads, 2 * dh), lambda b, s, pt: (b, 0, 0))]
        + [page_spec(u) for u in range(n_sub)] + [page_spec(u) for u in range(n_sub)]
        + [pl.BlockSpec((2 * n_heads, rows), const2),
           pl.BlockSpec((2 * n_heads, 1), const2),
           pl.BlockSpec((1, n_heads, 2 * dh), lambda b, s, pt: (b, 0, 0)),
           pl.BlockSpec((1, n_heads, dv), lambda b, s, pt: (b, 0, 0)),
           pl.BlockSpec((4, dh), const2),
           pl.BlockSpec((1, dv), const2)],
        out_specs=pl.BlockSpec((1, n_heads, dv), lambda b, s, pt: (b, 0, 0)),
        scratch_shapes=[pltpu.VMEM((2 * n_heads, 1), F32), pltpu.VMEM((2 * n_heads, 1), F32),
                        pltpu.VMEM((2 * n_heads, dv), F32)],
    )
    kern = functools.partial(_decode_attn_kernel, n_sub=n_sub, n_heads=n_heads, lam_init=lam_init)
    return pl.pallas_call(
        kern, grid_spec=grid_spec,
        out_shape=jax.ShapeDtypeStruct((nb, n_heads, dv), F32),
        compiler_params=_cparams(("parallel", "arbitrary")),
        name="diff_attn_decode",
    )(pt_flat, qall, *([cache_k] * n_sub), *([cache_v] * n_sub), bias_last, bias_own, kn, vn,
      lam_vecs, diff_g.reshape(1, dv).astype(F32))


def _gla_step_kernel(s_ref, a_ref, k_ref, q_ref, v_ref, r_ref, gn_ref, s_out_ref, y_ref):
    s_new = jnp.exp(a_ref[0, 0]) * s_ref[...] + k_ref[0, 0] * v_ref[0, 0]
    s_out_ref[0, 0] = s_new
    o = jnp.sum(q_ref[0, 0] * s_new, axis=0, keepdims=True)
    ms = jnp.mean(o * o, axis=-1, keepdims=True)
    rr = r_ref[0, 0]
    y_ref[0, 0] = o * lax.rsqrt(ms + EPS) * gn_ref[...] * (rr * jax.nn.sigmoid(rr))


def decode_gla(layer, state, log_a, k, q, v, r, gla_g):
    _, nb, n_heads, dk, dv = state.shape
    col = lambda t: t.reshape(nb, n_heads, dk, 1)
    rowv = lambda t: t.reshape(nb, n_heads, 1, dv)
    cspec = pl.BlockSpec((1, 1, dk, 1), lambda b, h: (b, h, 0, 0))
    rspec = pl.BlockSpec((1, 1, 1, dv), lambda b, h: (b, h, 0, 0))
    return pl.pallas_call(
        _gla_step_kernel,
        grid=(nb, n_heads),
        in_specs=[pl.BlockSpec((None, None, None, dk, dv), lambda b, h: (layer, b, h, 0, 0)),
                  cspec, cspec, cspec, rspec, rspec,
                  pl.BlockSpec((1, dv), lambda b, h: (0, 0))],
        out_specs=[pl.BlockSpec((1, 1, dk, dv), lambda b, h: (b, h, 0, 0)), rspec],
        out_shape=[jax.ShapeDtypeStruct((nb, n_heads, dk, dv), F32),
                   jax.ShapeDtypeStruct((nb, n_heads, 1, dv), F32)],
        compiler_params=_cparams(("parallel", "parallel")),
        name="gla_decode",
    )(state, col(log_a), col(k), col(q), rowv(v), rowv(r), gla_g.reshape(1, dv).astype(F32))


def _layer_weights(i, w_in, sizes, others):
    offs = np.concatenate([[0], np.cumsum(sizes)])
    wi = w_in[i]
    seg = lambda a, b: wi[:, offs[a]:offs[b]].astype(BF16)
    w = {"q": seg(0, 1), "k": seg(1, 2), "v": seg(2, 3),
         "rest": jnp.concatenate([wi[:, offs[3]:offs[7]], wi[:, offs[8]:offs[10]]], axis=1).astype(BF16),
         "glow": seg(7, 8)}
    for name, arr in others.items():
        w[name] = arr[i].astype(BF16)
    return w


def _dense_front(x2, lw, g_mix, q_g, k_g, gate_w2, gate_b, dh, tm, stacks=None):
    h = rmsnorm_rows(x2, g_mix)
    d_qk = lw["q"].shape[1]
    d_v = lw["v"].shape[1]
    qn = functools.partial(_ep_groupnorm, group=dh, scale=dh ** -0.5 * LOG2E)
    kn = functools.partial(_ep_groupnorm, group=dh, scale=1.0)
    gq = (q_g.reshape(1, dh).astype(F32), "full", 0)
    gk = (k_g.reshape(1, dh).astype(F32), "full", 0)
    if stacks is not None:
        layer, kbuf, vbuf = stacks
        (q,) = fused_matmul([h], [(0, lw["q"])], [gq], [(d_qk, BF16, "t")], qn, tm=tm, name="in_q")
        kbuf, k16 = fused_matmul([h], [(0, lw["k"])], [gk], [(d_qk, F32, "n"), (d_qk, BF16, "n")], kn,
                                 tm=tm, stacked=(0, layer, kbuf), name="in_k")
        vbuf, v16 = fused_matmul([h], [(0, lw["v"])], [], [(d_v, F32, "n"), (d_v, BF16, "t")], _ep_cast,
                                 tm=tm, stacked=(0, layer, vbuf), name="in_v")
        kv = (kbuf, k16, vbuf, v16)
    else:
        (q,) = fused_matmul([h], [(0, lw["q"])], [gq], [(d_qk, BF16, "n")], qn, tm=tm, name="in_q")
        (k32,) = fused_matmul([h], [(0, lw["k"])], [gk], [(d_qk, F32, "n")], kn, tm=tm, name="in_k")
        (v32,) = fused_matmul([h], [(0, lw["v"])], [], [(d_v, F32, "n")], _ep_cast, tm=tm, name="in_v")
        kv = (k32, v32)
    (rest,) = fused_matmul([h], [(0, lw["rest"])], [], [(lw["rest"].shape[1], BF16, "n")], _ep_cast, tm=tm,
                           name="in_rest")
    n_gate = gate_w2.shape[1]
    (log_a,) = fused_matmul([h], [(0, lw["glow"])],
                            [(gate_w2.astype(F32), "full", 0), (gate_b.reshape(1, n_gate).astype(F32), "full", 0)],
                            [(n_gate, F32, "whole")], _ep_gate, tm=tm, name="in_gate")
    return q, kv, rest, log_a


def _dense_back(x2, ya, yb, rest, off_ga, off_gb, p2, lw, g_ffn, g_ple, tm):
    d = x2.shape[1]
    tn = _tile(d, 1024)
    tn_m = _tile(d, 512)
    assert off_ga % tn_m == 0 and off_gb % tn_m == 0
    (merged,) = fused_matmul([ya, yb], [(0, lw["w_pa"]), (1, lw["w_pb"])],
                             [(rest, "tile", off_ga // tn_m), (rest, "tile", off_gb // tn_m)],
                             [(d, BF16, "n")], _ep_merge, tm=tm, tn=tn_m, name="merge")
    (x2,) = fused_matmul([merged], [(0, lw["w_o"])], [(x2, "tile", 0)], [(d, F32, "n")], _ep_residual,
                         tm=tm, tn=tn, name="out_proj")
    h = rmsnorm_rows(x2, g_ffn)
    d_ff = lw["w_gate"].shape[1]
    (gu,) = fused_matmul([h], [(0, lw["w_gate"]), (0, lw["w_up"])], [], [(d_ff, BF16, "n")], _ep_swiglu,
                         tm=tm, tn=512, name="ffn_up")
    (x2,) = fused_matmul([gu], [(0, lw["w_down"])], [(x2, "tile", 0)], [(d, F32, "n")], _ep_residual,
                         tm=min(tm, 512), tn=512, name="ffn_down")
    h = rmsnorm_rows(x2, g_ple)
    (x2,) = fused_matmul([h, p2], [(0, lw["w_ple_gate"]), (1, lw["w_ple_proj"])], [(x2, "tile", 0)],
                         [(d, F32, "n")], _ep_ple, tm=tm, tn=tn, name="ple")
    return x2


def kernel(x_prompt, x_sample, cache_k, cache_v, state_gla, page_table, p_prompt, p_sample, rel_bias, norm_mix_g, w_in, gla_gate_w2, gla_gate_b, q_norm_g, k_norm_g, lambda_q1, lambda_k1, lambda_q2, lambda_k2, diff_norm_g, gla_norm_g, w_pa, w_pb, w_o, norm_ffn_g, w_gate, w_up, w_down, norm_ple_g, w_ple_gate, w_ple_proj):
    depth = w_in.shape[0]
    bp, lp, d = x_prompt.shape
    bs, ls, _ = x_sample.shape
    assert ls == 1
    _, _, page, h_a, dv_a = cache_v.shape
    dh_a = dv_a // 2
    d_a = h_a * dv_a
    _, _, h_b, dk_b, dv_b = state_gla.shape
    d_b = h_b * dv_b
    rank = gla_gate_w2.shape[1]
    sizes = [d_a, d_a, d_a, h_b * dk_b, h_b * dk_b, d_b, d_b, rank, d, d]
    off_qb, off_kb = 0, h_b * dk_b
    off_vb = 2 * h_b * dk_b
    off_r = off_vb + d_b
    off_ga = off_r + d_b
    off_gb = off_ga + d
    others = {"w_pa": w_pa, "w_pb": w_pb, "w_o": w_o, "w_gate": w_gate, "w_up": w_up, "w_down": w_down,
              "w_ple_gate": w_ple_gate, "w_ple_proj": w_ple_proj}

    mp = bp * lp
    ms = SAMPLE_ROWS
    xp = x_prompt.reshape(mp, d)
    xs = jnp.zeros((ms, d), F32).at[:bs].set(x_sample.reshape(bs, d))
    attn_bias = _prompt_bias_tiles(rel_bias, _tile(lp, ATTN_TILE))
    dec_bias_last, dec_bias_own = _decode_bias(rel_bias, page, h_a)
    kbuf = jnp.zeros((depth, mp, d_a), F32)
    vbuf = jnp.zeros((depth, mp, d_a), F32)

    sp_l, ks_l, vs_l, ss_l = [], [], [], []
    for i in range(depth):
        lam_init = 0.8 - 0.6 * math.exp(-0.3 * i)
        lw = _layer_weights(i, w_in, sizes, others)
        lam_vecs = jnp.stack([lambda_q1[i], lambda_k1[i], lambda_q2[i], lambda_k2[i]]).astype(F32)

        qt, (kbuf, k16, vbuf, vt), rest, log_a = _dense_front(
            xp, lw, norm_mix_g[i], q_norm_g[i], k_norm_g[i], gla_gate_w2[i], gla_gate_b[i], dh_a, ROW_TILE,
            stacks=(i, kbuf, vbuf))
        to3 = lambda t: t.reshape(bp, lp, t.shape[1])
        ya = prompt_attention(qt, to3(k16), vt, attn_bias, lam_vecs, diff_norm_g[i], lam_init, h_a, bp, lp)
        yb, st = prompt_gla(to3(rest), to3(log_a), gla_norm_g[i], h_b, dk_b, dv_b, off_qb, off_kb, off_vb, off_r)
        xp = _dense_back(xp, ya.reshape(mp, d_a), yb.reshape(mp, d_b), rest, off_ga, off_gb,
                         p_prompt[i].reshape(mp, -1), lw, norm_ffn_g[i], norm_ple_g[i], ROW_TILE)
        sp_l.append(jnp.swapaxes(st, 2, 3))

        q, (k32, v32), rest, log_a = _dense_front(
            xs, lw, norm_mix_g[i], q_norm_g[i], k_norm_g[i], gla_gate_w2[i], gla_gate_b[i], dh_a, ms)
        ya = decode_attention(i, q[:bs].astype(F32), k32[:bs], v32[:bs], cache_k, cache_v, page_table,
                              dec_bias_last, dec_bias_own, lam_vecs, diff_norm_g[i], lam_init)
        rest32 = rest[:bs].astype(F32)
        hk = lambda t: t.reshape(bs, h_b, -1)
        s_new, yb = decode_gla(i, state_gla, hk(log_a[:bs]),
                               hk(rest32[:, off_kb:off_kb + h_b * dk_b]),
                               hk(rest32[:, off_qb:off_qb + h_b * dk_b]) * (dk_b ** -0.5),
                               hk(rest32[:, off_vb:off_vb + d_b]), hk(rest32[:, off_r:off_r + d_b]),
                               gla_norm_g[i])
        pad = lambda t: jnp.zeros((ms, t.shape[1]), BF16).at[:bs].set(t.astype(BF16))
        ps = jnp.zeros((ms, p_sample.shape[-1]), F32).at[:bs].set(p_sample[i].reshape(bs, -1))
        xs = _dense_back(xs, pad(ya.reshape(bs, d_a)), pad(yb.reshape(bs, d_b)), rest, off_ga, off_gb,
                         ps, lw, norm_ffn_g[i], norm_ple_g[i], ms)
        ks_l.append(k32[:bs].reshape(bs, 1, h_a, dv_a))
        vs_l.append(v32[:bs].reshape(bs, 1, h_a, dv_a))
        ss_l.append(s_new)

    return (xp.reshape(bp, lp, d), xs[:bs].reshape(bs, 1, d),
            kbuf.reshape(depth, bp, lp, h_a, dv_a), vbuf.reshape(depth, bp, lp, h_a, dv_a),
            jnp.stack(sp_l), jnp.stack(ks_l), jnp.stack(vs_l), jnp.stack(ss_l))
```

```python
import functools
import math

import numpy as np
import jax
import jax.numpy as jnp
from jax import lax
from jax.experimental import pallas as pl
from jax.experimental.pallas import tpu as pltpu

EPS = 1e-6
GATE_TAU = 16.0
GLA_CHUNK = 64
MAX_EXACT = 16
MAX_DISTANCE = 128
LOG2E = math.log2(math.e)
NEG = -1e30
VMEM_LIMIT = 56 * 1024 * 1024
ATTN_TILE = 512
GLA_BLOCK = 512
ROW_TILE = 1024
COL_TILE_NARROW = 512
SAMPLE_ROWS = 16
DECODE_PAGES = 8

BF16 = jnp.bfloat16
F32 = jnp.float32


def _cparams(sem):
    return pltpu.CompilerParams(dimension_semantics=sem, vmem_limit_bytes=VMEM_LIMIT)


def _tile(n, pref):
    if n <= pref:
        return n
    t = pref
    while n % t:
        t //= 2
    return t


def _rmsnorm_kernel(x_ref, g_ref, o_ref):
    x = x_ref[...]
    ms = jnp.mean(x * x, axis=-1, keepdims=True)
    o_ref[...] = (x * lax.rsqrt(ms + EPS) * g_ref[...]).astype(o_ref.dtype)


def rmsnorm_rows(x, g):
    m, d = x.shape
    tm = _tile(m, 512)
    return pl.pallas_call(
        _rmsnorm_kernel,
        grid=(m // tm,),
        in_specs=[pl.BlockSpec((tm, d), lambda i: (i, 0)),
                  pl.BlockSpec((1, d), lambda i: (0, 0))],
        out_specs=pl.BlockSpec((tm, d), lambda i: (i, 0)),
        out_shape=jax.ShapeDtypeStruct((m, d), BF16),
        compiler_params=_cparams(("parallel",)),
        name="rmsnorm",
    )(x, g.reshape(1, d).astype(F32))


class _Out:
    def __init__(self, ref, transposed):
        self.ref = ref
        self.transposed = transposed

    def put(self, val, c0=None, c1=None):
        if self.transposed:
            val = val.T.astype(self.ref.dtype)
            if c0 is None:
                self.ref[...] = val
            else:
                self.ref[c0:c1, :] = val
        else:
            val = val.astype(self.ref.dtype)
            if c0 is None:
                self.ref[...] = val
            else:
                self.ref[:, c0:c1] = val


def _mm_kernel(*refs, pair_a, n_a, n_extra, out_t, epilogue):
    n_w = len(pair_a)
    n_out = len(out_t)
    a_refs = refs[:n_a]
    w_refs = refs[n_a:n_a + n_w]
    extras = refs[n_a + n_w:n_a + n_w + n_extra]
    out_refs = refs[len(refs) - n_w - n_out:len(refs) - n_w]
    w_bf16 = refs[len(refs) - n_w:]

    @pl.when(pl.program_id(1) == 0)
    def _():
        for w, sc in zip(w_refs, w_bf16):
            sc[...] = w[...].astype(BF16)

    a_vals = [a[...].astype(BF16) for a in a_refs]
    accs = [jnp.dot(a_vals[ai], sc[...], preferred_element_type=F32)
            for ai, sc in zip(pair_a, w_bf16)]
    epilogue(accs, extras, [_Out(r, t) for r, t in zip(out_refs, out_t)])


def fused_matmul(a_list, pairs, n, extras, outs, epilogue, *, tm, tn=1024, stacked=None, name="mm"):
    m = a_list[0].shape[0]
    tm = _tile(m, tm)
    tn = _tile(n, tn)
    grid = (n // tn, m // tm)
    in_specs, args, scratch = [], [], []
    for a in a_list:
        in_specs.append(pl.BlockSpec((tm, a.shape[1]), lambda j, i: (i, 0)))
        args.append(a)
    for _, w, layer, col0 in pairs:
        assert col0 % tn == 0
        k_dim = w.shape[-2]
        if w.ndim == 3:
            in_specs.append(pl.BlockSpec((None, k_dim, tn),
                                         lambda j, i, layer=layer, c=col0 // tn: (layer, 0, c + j)))
        else:
            in_specs.append(pl.BlockSpec((k_dim, tn), lambda j, i, c=col0 // tn: (0, c + j)))
        args.append(w)
        scratch.append(pltpu.VMEM((k_dim, tn), BF16))
    for arr, kind, off in extras:
        if kind == "tile":
            in_specs.append(pl.BlockSpec((tm, tn), lambda j, i, off=off: (i, off + j)))
        else:
            in_specs.append(pl.BlockSpec(arr.shape, lambda j, i, nd=arr.ndim: (0,) * nd))
        args.append(arr)
    out_specs, out_shape, aliases = [], [], {}
    for oi, (width, dt, kind) in enumerate(outs):
        if stacked is not None and stacked[0] == oi:
            _, layer, depth, buf = stacked
            assert kind == "n"
            out_specs.append(pl.BlockSpec((None, tm, tn), lambda j, i, layer=layer: (layer, i, j)))
            out_shape.append(jax.ShapeDtypeStruct((depth, m, n), dt))
            if buf is not None:
                assert buf.shape == (depth, m, n) and buf.dtype == dt
                in_specs.append(pl.BlockSpec(memory_space=pl.ANY))
                aliases[len(args)] = oi
                args.append(buf)
        elif kind == "n":
            out_specs.append(pl.BlockSpec((tm, tn), lambda j, i: (i, j)))
            out_shape.append(jax.ShapeDtypeStruct((m, n), dt))
        elif kind == "t":
            out_specs.append(pl.BlockSpec((tn, tm), lambda j, i: (j, i)))
            out_shape.append(jax.ShapeDtypeStruct((n, m), dt))
        else:
            assert grid[0] == 1
            out_specs.append(pl.BlockSpec((tm, width), lambda j, i: (i, 0)))
            out_shape.append(jax.ShapeDtypeStruct((m, width), dt))
    kern = functools.partial(_mm_kernel, pair_a=tuple(p[0] for p in pairs), n_a=len(a_list),
                             n_extra=len(extras), out_t=tuple(o[2] == "t" for o in outs),
                             epilogue=epilogue)
    return pl.pallas_call(
        kern, grid=grid, in_specs=in_specs, out_specs=out_specs, out_shape=out_shape,
        scratch_shapes=scratch, input_output_aliases=aliases,
        compiler_params=_cparams(("parallel", "arbitrary")), name=name,
    )(*args)


def _ep_cast(accs, extras, outs):
    for o in outs:
        o.put(accs[0])


def _ep_groupnorm(accs, extras, outs, *, group, scale):
    acc = accs[0]
    g = extras[0][...] * scale
    for s in range(acc.shape[1] // group):
        z = acc[:, s * group:(s + 1) * group]
        ms = jnp.mean(z * z, axis=-1, keepdims=True)
        y = z * lax.rsqrt(ms + EPS) * g
        for o in outs:
            o.put(y, s * group, (s + 1) * group)


def _ep_gate(accs, extras, outs):
    y = jnp.dot(accs[0], extras[0][...], preferred_element_type=F32,
                precision=lax.Precision.HIGHEST) + extras[1][...]
    ls = jnp.minimum(y, 0.0) - jnp.log1p(jnp.exp(-jnp.abs(y)))
    outs[0].put(ls * (1.0 / GATE_TAU))


def _ep_merge(accs, extras, outs):
    ga = extras[0][...].astype(F32)
    gb = extras[1][...].astype(F32)
    outs[0].put(jax.nn.sigmoid(ga) * accs[0] + jax.nn.sigmoid(gb) * accs[1])


def _ep_residual(accs, extras, outs):
    outs[0].put(extras[0][...] + accs[0])


def _ep_swiglu(accs, extras, outs):
    g = accs[0]
    outs[0].put(g * jax.nn.sigmoid(g) * accs[1])


def _ep_ple(accs, extras, outs):
    outs[0].put(extras[0][...] + jax.nn.sigmoid(accs[0]) * accs[1])


def _bucket_table(n, n_buckets):
    d = np.arange(n)
    nf = np.maximum(d, 1).astype(np.float64)
    large = MAX_EXACT + (np.log(nf / MAX_EXACT) / math.log(MAX_DISTANCE / MAX_EXACT)
                         * (n_buckets - MAX_EXACT)).astype(np.int64)
    large = np.minimum(large, n_buckets - 1)
    return np.where(d < MAX_EXACT, d, large).astype(np.int32)


def _toeplitz(w, t):
    n = 2 * t
    big = jnp.tile(w, (1, t))[:, :t * (n - 1)].reshape(w.shape[0], t, n - 1)
    return big[:, :, :t]


def _prompt_bias_tiles(rel_bias, t):
    n_buckets = rel_bias.shape[0]
    assert t + 1 >= MAX_DISTANCE
    table = _bucket_table(2 * t, n_buckets)
    rb = rel_bias.astype(F32)
    f = ((rb[table] - rb[n_buckets - 1][None, :]) * LOG2E).T
    neg = jnp.full((f.shape[0], t), NEG, F32)
    diag = _toeplitz(jnp.concatenate([f[:, :t], neg], axis=1), t)
    sub = _toeplitz(jnp.concatenate([f[:, t:], f[:, :t]], axis=1), t)
    return jnp.stack([diag, sub], axis=1)


def _decode_bias(rel_bias, page, n_heads):
    n_buckets = rel_bias.shape[0]
    assert page + 1 >= MAX_DISTANCE
    table = _bucket_table(page + 1, n_buckets)
    rel = (rel_bias.astype(F32) - rel_bias.astype(F32)[n_buckets - 1][None, :]) * LOG2E
    head_of_row = np.arange(2 * n_heads) % n_heads
    by_dist = rel[table][:, head_of_row].T
    last = jnp.repeat(by_dist[:, page - np.arange(page)], n_heads, axis=1)
    own = by_dist[:, 0:1]
    return last, own


def _attn_kernel(q_ref, k_ref, v_ref, bias_ref, lam_ref, g_ref, o_ref, m_sc, l_sc, acc_sc,
                 *, lam_init, dh, t):
    qi = pl.program_id(2)
    m_sc[...] = jnp.full(m_sc.shape, NEG, F32)
    l_sc[...] = jnp.zeros(l_sc.shape, F32)
    acc_sc[...] = jnp.zeros(acc_sc.shape, F32)

    def tile(j, bias):
        ks = pl.ds(pl.multiple_of(j * t, t), t)
        vt = v_ref[:, ks]
        scores = [jnp.dot(k_ref[0, ks, c * dh:(c + 1) * dh],
                          q_ref[c * dh:(c + 1) * dh, :],
                          preferred_element_type=F32) for c in range(2)]
        probs, alphas = [], []
        for c in range(2):
            s = scores[c]
            if bias is not None:
                s = s + bias
            m_prev = m_sc[c]
            m_new = jnp.maximum(m_prev, jnp.max(s, axis=0, keepdims=True))
            alpha = jnp.exp2(m_prev - m_new)
            pr = jnp.exp2(s - m_new)
            l_sc[c] = alpha * l_sc[c] + jnp.sum(pr, axis=0, keepdims=True)
            m_sc[c] = m_new
            probs.append(pr.astype(BF16))
            alphas.append(alpha)
        for c in range(2):
            acc_sc[c] = alphas[c] * acc_sc[c] + jnp.dot(vt, probs[c], preferred_element_type=F32)

    def far(j, carry):
        tile(j, None)
        return carry

    lax.fori_loop(0, qi - 1, far, 0)

    @pl.when(qi >= 1)
    def _():
        tile(qi - 1, bias_ref[0, 1])

    tile(qi, bias_ref[0, 0])

    lv = lam_ref[...]
    lam = (jnp.exp(jnp.sum(lv[0:1] * lv[1:2], axis=-1, keepdims=True))
           - jnp.exp(jnp.sum(lv[2:3] * lv[3:4], axis=-1, keepdims=True)) + lam_init)
    r0 = 1.0 / l_sc[0]
    r1 = lam / l_sc[1]
    ot = acc_sc[0] * r0 - acc_sc[1] * r1
    ms = jnp.mean(ot * ot, axis=0, keepdims=True)
    yt = ot * lax.rsqrt(ms + EPS) * (g_ref[...] * (1.0 - lam_init))
    o_ref[0] = yt.T.astype(o_ref.dtype)


def prompt_attention(qt, k, vt, bias, lam_vecs, diff_g, lam_init, n_heads, b, l):
    dtot = k.shape[2]
    dv = dtot // n_heads
    dh = dv // 2
    t = bias.shape[-1]
    nq = l // t
    kern = functools.partial(_attn_kernel, lam_init=lam_init, dh=dh, t=t)
    return pl.pallas_call(
        kern,
        grid=(b, n_heads, nq),
        in_specs=[
            pl.BlockSpec((dv, t), lambda bi, h, qi: (h, bi * nq + qi)),
            pl.BlockSpec((1, l, dv), lambda bi, h, qi: (bi, 0, h)),
            pl.BlockSpec((dv, l), lambda bi, h, qi: (h, bi)),
            pl.BlockSpec((1, 2, t, t), lambda bi, h, qi: (h, 0, 0, 0)),
            pl.BlockSpec((4, dh), lambda bi, h, qi: (0, 0)),
            pl.BlockSpec((dv, 1), lambda bi, h, qi: (0, 0)),
        ],
        out_specs=pl.BlockSpec((1, t, dv), lambda bi, h, qi: (bi, qi, h)),
        out_shape=jax.ShapeDtypeStruct((b, l, dtot), BF16),
        scratch_shapes=[pltpu.VMEM((2, 1, t), F32), pltpu.VMEM((2, 1, t), F32),
                        pltpu.VMEM((2, dv, t), F32)],
        compiler_params=_cparams(("parallel", "parallel", "arbitrary")),
        name="diff_attn_prompt",
    )(qt, k, vt, bias, lam_vecs, diff_g.reshape(dv, 1).astype(F32))


def _gla_kernel(q_ref, k_ref, v_ref, r_ref, g_ref, gn_ref, y_ref, s_ref, st_sc, *, chunk, nchunk, qscale):
    li = pl.program_id(1)
    nb = q_ref.shape[0]

    @pl.when(li == 0)
    def _():
        st_sc[...] = jnp.zeros(st_sc.shape, F32)

    lb = nchunk * chunk
    row = lax.broadcasted_iota(jnp.int32, (chunk, chunk), 0)
    col = lax.broadcasted_iota(jnp.int32, (chunk, chunk), 1)
    tril = (row >= col).astype(BF16)
    brow = lax.broadcasted_iota(jnp.int32, (lb, lb), 0)
    bcol = lax.broadcasted_iota(jnp.int32, (lb, lb), 1)
    block_causal = (brow >= bcol) & (brow // chunk == bcol // chunk)
    gn = gn_ref[...]
    nt = (((1,), (1,)), ((), ()))
    tn = (((0,), (0,)), ((), ()))

    pre = []
    for b in range(nb):
        g = g_ref[b]
        g_hi = g.astype(BF16)
        r1 = g - g_hi.astype(F32)
        g_mid = r1.astype(BF16)
        g_lo = (r1 - g_mid.astype(F32)).astype(BF16)
        cums, tots = [], []
        for c in range(nchunk):
            sl = slice(c * chunk, (c + 1) * chunk)
            bc = (jnp.dot(tril, g_hi[sl], preferred_element_type=F32)
                  + jnp.dot(tril, g_mid[sl], preferred_element_type=F32)
                  + jnp.dot(tril, g_lo[sl], preferred_element_type=F32))
            cums.append(bc)
            tots.append(jnp.broadcast_to(bc[chunk - 1:chunk, :], bc.shape))
        bcum = jnp.concatenate(cums, axis=0)
        btot = jnp.concatenate(tots, axis=0)
        q = q_ref[b].astype(F32) * qscale
        k = k_ref[b].astype(F32)
        v = v_ref[b]
        qt = (q * jnp.exp(bcum)).astype(BF16)
        kt = (k * jnp.exp(-bcum)).astype(BF16)
        k2 = (k * jnp.exp(btot - bcum)).astype(BF16)
        a = lax.dot_general(qt, kt, nt, preferred_element_type=F32)
        a = jnp.where(block_causal, a, 0.0)
        o_intra = jnp.dot(a.astype(BF16), v, preferred_element_type=F32)
        pre.append((qt, k2, v, o_intra, jnp.exp(btot)))

    states = [st_sc[b] for b in range(nb)]
    outs = [[] for _ in range(nb)]
    for c in range(nchunk):
        sl = slice(c * chunk, (c + 1) * chunk)
        for b in range(nb):
            qt, k2, v, o_intra, decay = pre[b]
            st = states[b]
            outs[b].append(o_intra[sl] + lax.dot_general(qt[sl], st.astype(BF16), nt,
                                                         preferred_element_type=F32))
            states[b] = st * decay[c * chunk:c * chunk + 1, :] + lax.dot_general(
                v[sl], k2[sl], tn, preferred_element_type=F32)

    for b in range(nb):
        st_sc[b] = states[b]
        o = jnp.concatenate(outs[b], axis=0)
        ms = jnp.mean(o * o, axis=-1, keepdims=True)
        rr = r_ref[b].astype(F32)
        y = o * lax.rsqrt(ms + EPS) * gn * (rr * jax.nn.sigmoid(rr))
        y_ref[b] = y.astype(y_ref.dtype)

    @pl.when(li == pl.num_programs(1) - 1)
    def _():
        s_ref[:, 0] = st_sc[...]


def prompt_gla(z, log_a, gla_g, n_heads, dk, dv, off_q, off_k, off_v, off_r):
    b, l, _ = z.shape
    chunk = math.gcd(l, GLA_CHUNK)
    lb = _tile(l, GLA_BLOCK)
    nchunk = lb // chunk
    kern = functools.partial(_gla_kernel, chunk=chunk, nchunk=nchunk, qscale=dk ** -0.5)
    oq, ok, ov, orr = off_q // dk, off_k // dk, off_v // dv, off_r // dv
    return pl.pallas_call(
        kern,
        grid=(n_heads, l // lb),
        in_specs=[
            pl.BlockSpec((b, lb, dk), lambda h, i: (0, i, oq + h)),
            pl.BlockSpec((b, lb, dk), lambda h, i: (0, i, ok + h)),
            pl.BlockSpec((b, lb, dv), lambda h, i: (0, i, ov + h)),
            pl.BlockSpec((b, lb, dv), lambda h, i: (0, i, orr + h)),
            pl.BlockSpec((b, lb, dk), lambda h, i: (0, i, h)),
            pl.BlockSpec((1, dv), lambda h, i: (0, 0)),
        ],
        out_specs=[
            pl.BlockSpec((b, lb, dv), lambda h, i: (0, i, h)),
            pl.BlockSpec((b, 1, dv, dk), lambda h, i: (0, h, 0, 0)),
        ],
        out_shape=[jax.ShapeDtypeStruct((b, l, n_heads * dv), BF16),
                   jax.ShapeDtypeStruct((b, n_heads, dv, dk), F32)],
        scratch_shapes=[pltpu.VMEM((b, dv, dk), F32)],
        compiler_params=_cparams(("parallel", "arbitrary")),
        name="gla_prompt",
    )(z, z, z, z, log_a, gla_g.reshape(1, dv).astype(F32))


def _decode_attn_kernel(pt_ref, q_ref, *refs, n_sub, n_heads, lam_init):
    k_refs = refs[:n_sub]
    v_refs = refs[n_sub:2 * n_sub]
    (bias_last_ref, own_ref, kn_ref, vn_ref, lam_ref, g_ref, o_ref, m_sc, l_sc, acc_sc) = refs[2 * n_sub:]
    s_idx = pl.program_id(1)
    last = pl.num_programs(1) - 1

    @pl.when(s_idx == 0)
    def _():
        m_sc[...] = jnp.full(m_sc.shape, NEG, F32)
        l_sc[...] = jnp.zeros(l_sc.shape, F32)
        acc_sc[...] = jnp.zeros(acc_sc.shape, F32)

    q = q_ref[0]
    nrow = q.shape[0]

    def update(kmats, vmats, biases):
        parts = []
        for kmat, bias in zip(kmats, biases):
            sp = lax.dot_general(q, kmat, (((1,), (1,)), ((), ())), preferred_element_type=F32)
            parts.append(sp if bias is None else sp + bias)
        s = parts[0] if len(parts) == 1 else jnp.concatenate(parts, axis=1)
        r = s.shape[1]
        rowh = lax.broadcasted_iota(jnp.int32, (nrow, r), 0) % n_heads
        colh = lax.broadcasted_iota(jnp.int32, (nrow, r), 1) % n_heads
        s = jnp.where(rowh == colh, s, NEG)
        m_prev = m_sc[...]
        m_new = jnp.maximum(m_prev, jnp.max(s, axis=-1, keepdims=True))
        alpha = jnp.exp2(m_prev - m_new)
        pr = jnp.exp2(s - m_new).astype(BF16)
        l_sc[...] = alpha * l_sc[...] + jnp.sum(pr.astype(F32), axis=-1, keepdims=True)
        pv = None
        off = 0
        for vmat in vmats:
            d = jnp.dot(pr[:, off:off + vmat.shape[0]], vmat, preferred_element_type=F32)
            pv = d if pv is None else pv + d
            off += vmat.shape[0]
        acc_sc[...] = alpha * acc_sc[...] + pv
        m_sc[...] = m_new

    kmats, vmats = [], []
    for u in range(n_sub):
        kp = k_refs[u][...]
        vp = v_refs[u][...]
        kmats.append(kp.reshape(kp.shape[0] * kp.shape[1], kp.shape[2]).astype(BF16))
        vmats.append(vp.reshape(vp.shape[0] * vp.shape[1], vp.shape[2]).astype(BF16))
    is_last = (s_idx == last).astype(F32)
    update(kmats, vmats, [None] * (n_sub - 1) + [bias_last_ref[...] * is_last])

    @pl.when(s_idx == last)
    def _():
        update([kn_ref[0]], [vn_ref[0]], [own_ref[...]])
        lv = lam_ref[...]
        lam = (jnp.exp(jnp.sum(lv[0:1] * lv[1:2], axis=-1, keepdims=True))
               - jnp.exp(jnp.sum(lv[2:3] * lv[3:4], axis=-1, keepdims=True)) + lam_init)
        on = acc_sc[...] / l_sc[...]
        o = on[:n_heads] - lam * on[n_heads:]
        ms = jnp.mean(o * o, axis=-1, keepdims=True)
        o_ref[0] = o * lax.rsqrt(ms + EPS) * g_ref[...] * (1.0 - lam_init)


def decode_attention(layer, q, k_new, v_new, cache_k, cache_v, page_table, bias_last, bias_own, lam_vecs,
                     diff_g, lam_init):
    nb, n_pages = page_table.shape
    _, _, page, n_heads, dv = cache_v.shape
    dh = dv // 2
    n_sub = DECODE_PAGES if n_pages % DECODE_PAGES == 0 else 1
    n_steps = n_pages // n_sub
    rows = page * n_heads
    qh = q.reshape(nb, n_heads, 2 * dh)
    half = (np.arange(2 * dh) // dh)[None, :] == np.arange(2)[:, None]
    qall = (qh[:, None, :, :] * jnp.asarray(half, F32)[None, :, None, :]).reshape(nb, 2 * n_heads, 2 * dh)
    qall = qall.astype(BF16)
    kn = k_new.reshape(nb, n_heads, 2 * dh).astype(BF16)
    vn = v_new.reshape(nb, n_heads, dv).astype(BF16)
    pt_flat = page_table.reshape(-1).astype(jnp.int32)

    def page_spec(u):
        return pl.BlockSpec((None, None, page, n_heads, dv),
                            lambda b, s, pt, u=u: (layer, pt[b * n_pages + s * n_sub + u], 0, 0, 0))

    const2 = lambda b, s, pt: (0, 0)
    grid_spec = pltpu.PrefetchScalarGridSpec(
        num_scalar_prefetch=1,
        grid=(nb, n_steps),
        in_specs=[pl.BlockSpec((1, 2 * n_heads, 2 * dh), lambda b, s, pt: (b, 0, 0))]
        + [page_spec(u) for u in range(n_sub)] + [page_spec(u) for u in range(n_sub)]
        + [pl.BlockSpec((2 * n_heads, rows), const2),
           pl.BlockSpec((2 * n_heads, 1), const2),
           pl.BlockSpec((1, n_heads, 2 * dh), lambda b, s, pt: (b, 0, 0)),
           pl.BlockSpec((1, n_heads, dv), lambda b, s, pt: (b, 0, 0)),
           pl.BlockSpec((4, dh), const2),
           pl.BlockSpec((1, dv), const2)],
        out_specs=pl.BlockSpec((1, n_heads, dv), lambda b, s, pt: (b, 0, 0)),
        scratch_shapes=[pltpu.VMEM((2 * n_heads, 1), F32), pltpu.VMEM((2 * n_heads, 1), F32),
                        pltpu.VMEM((2 * n_heads, dv), F32)],
    )
    kern = functools.partial(_decode_attn_kernel, n_sub=n_sub, n_heads=n_heads, lam_init=lam_init)
    return pl.pallas_call(
        kern, grid_spec=grid_spec,
        out_shape=jax.ShapeDtypeStruct((nb, n_heads, dv), F32),
        compiler_params=_cparams(("parallel", "arbitrary")),
        name="diff_attn_decode",
    )(pt_flat, qall, *([cache_k] * n_sub), *([cache_v] * n_sub), bias_last, bias_own, kn, vn,
      lam_vecs, diff_g.reshape(1, dv).astype(F32))


def _gla_step_kernel(s_ref, a_ref, k_ref, q_ref, v_ref, r_ref, gn_ref, *rest):
    s_out_ref, y_ref = rest[-2:]
    s_new = jnp.exp(a_ref[0, 0]) * s_ref[...] + k_ref[0, 0] * v_ref[0, 0]
    s_out_ref[...] = s_new
    o = jnp.sum(q_ref[0, 0] * s_new, axis=0, keepdims=True)
    ms = jnp.mean(o * o, axis=-1, keepdims=True)
    rr = r_ref[0, 0]
    y_ref[0, 0] = o * lax.rsqrt(ms + EPS) * gn_ref[...] * (rr * jax.nn.sigmoid(rr))


def decode_gla(layer, state, new_states, log_a, k, q, v, r, gla_g):
    _, nb, n_heads, dk, dv = state.shape
    col = lambda t: t.reshape(nb, n_heads, dk, 1)
    rowv = lambda t: t.reshape(nb, n_heads, 1, dv)
    cspec = pl.BlockSpec((1, 1, dk, 1), lambda b, h: (b, h, 0, 0))
    rspec = pl.BlockSpec((1, 1, 1, dv), lambda b, h: (b, h, 0, 0))
    sspec = pl.BlockSpec((None, None, None, dk, dv), lambda b, h: (layer, b, h, 0, 0))
    in_specs = [sspec, cspec, cspec, cspec, rspec, rspec, pl.BlockSpec((1, dv), lambda b, h: (0, 0))]
    args = [state, col(log_a), col(k), col(q), rowv(v), rowv(r), gla_g.reshape(1, dv).astype(F32)]
    aliases = {}
    if new_states is not None:
        in_specs.append(pl.BlockSpec(memory_space=pl.ANY))
        aliases[len(args)] = 0
        args.append(new_states)
    return pl.pallas_call(
        _gla_step_kernel,
        grid=(nb, n_heads),
        in_specs=in_specs,
        out_specs=[sspec, rspec],
        out_shape=[jax.ShapeDtypeStruct(state.shape, F32),
                   jax.ShapeDtypeStruct((nb, n_heads, 1, dv), F32)],
        input_output_aliases=aliases,
        compiler_params=_cparams(("parallel", "parallel")),
        name="gla_decode",
    )(*args)


def _layer_weights(i, w_in, sizes, others):
    offs = [int(o) for o in np.concatenate([[0], np.cumsum(sizes)])]
    w = {"q": (w_in, i, offs[0], sizes[0]), "k": (w_in, i, offs[1], sizes[1]),
         "v": (w_in, i, offs[2], sizes[2]), "rest": (w_in, i, offs[3], offs[7] - offs[3]),
         "gates": (w_in[i][:, offs[8]:offs[10]], None, 0, offs[10] - offs[8]),
         "glow": (w_in[i][:, offs[7]:offs[8]], None, 0, sizes[7])}
    for name, arr in others.items():
        w[name] = (arr, i, 0, arr.shape[2])
    return w


def _pair(a_idx, wspec):
    return (a_idx, wspec[0], wspec[1], wspec[2])


def _dense_front(x2, lw, g_mix, q_g, k_g, gate_w2, gate_b, dh, tm, stacks=None):
    h = rmsnorm_rows(x2, g_mix)
    d_qk = lw["q"][3]
    d_v = lw["v"][3]
    qn = functools.partial(_ep_groupnorm, group=dh, scale=dh ** -0.5 * LOG2E)
    kn = functools.partial(_ep_groupnorm, group=dh, scale=1.0)
    gq = (q_g.reshape(1, dh).astype(F32), "full", 0)
    gk = (k_g.reshape(1, dh).astype(F32), "full", 0)
    if stacks is not None:
        layer, depth, kbuf, vbuf = stacks
        (q,) = fused_matmul([h], [_pair(0, lw["q"])], d_qk, [gq], [(d_qk, BF16, "t")], qn, tm=tm, name="in_q")
        kbuf, k16 = fused_matmul([h], [_pair(0, lw["k"])], d_qk, [gk], [(d_qk, F32, "n"), (d_qk, BF16, "n")],
                                 kn, tm=tm, stacked=(0, layer, depth, kbuf), name="in_k")
        vbuf, v16 = fused_matmul([h], [_pair(0, lw["v"])], d_v, [], [(d_v, F32, "n"), (d_v, BF16, "t")],
                                 _ep_cast, tm=tm, stacked=(0, layer, depth, vbuf), name="in_v")
        kv = (kbuf, k16, vbuf, v16)
    else:
        (q,) = fused_matmul([h], [_pair(0, lw["q"])], d_qk, [gq], [(d_qk, BF16, "n")], qn, tm=tm, name="in_q")
        (k32,) = fused_matmul([h], [_pair(0, lw["k"])], d_qk, [gk], [(d_qk, F32, "n")], kn, tm=tm, name="in_k")
        (v32,) = fused_matmul([h], [_pair(0, lw["v"])], d_v, [], [(d_v, F32, "n")], _ep_cast, tm=tm,
                              name="in_v")
        kv = (k32, v32)
    n_rest = lw["rest"][3]
    (rest,) = fused_matmul([h], [_pair(0, lw["rest"])], n_rest, [], [(n_rest, BF16, "n")], _ep_cast, tm=tm,
                           name="in_rest")
    n_gates = lw["gates"][3]
    (gates,) = fused_matmul([h], [_pair(0, lw["gates"])], n_gates, [], [(n_gates, BF16, "n")], _ep_cast,
                            tm=tm, name="in_gates")
    n_gate = gate_w2.shape[1]
    (log_a,) = fused_matmul([h], [_pair(0, lw["glow"])], lw["glow"][3],
                            [(gate_w2.astype(F32), "full", 0), (gate_b.reshape(1, n_gate).astype(F32), "full", 0)],
                            [(n_gate, F32, "whole")], _ep_gate, tm=tm, name="in_gate")
    return q, kv, rest, gates, log_a


def _dense_back(x2, ya, yb, gates, p2, lw, g_ffn, g_ple, tm):
    d = x2.shape[1]
    tn = _tile(d, COL_TILE_NARROW)
    (merged,) = fused_matmul([ya, yb], [_pair(0, lw["w_pa"]), _pair(1, lw["w_pb"])], d,
                             [(gates, "tile", 0), (gates, "tile", d // tn)],
                             [(d, BF16, "n")], _ep_merge, tm=tm, tn=tn, name="merge")
    (x2,) = fused_matmul([merged], [_pair(0, lw["w_o"])], d, [(x2, "tile", 0)], [(d, F32, "n")],
                         _ep_residual, tm=tm, tn=tn, name="out_proj")
    h = rmsnorm_rows(x2, g_ffn)
    d_ff = lw["w_gate"][3]
    (gu,) = fused_matmul([h], [_pair(0, lw["w_gate"]), _pair(0, lw["w_up"])], d_ff, [], [(d_ff, BF16, "n")],
                         _ep_swiglu, tm=tm, tn=COL_TILE_NARROW, name="ffn_up")
    (x2,) = fused_matmul([gu], [_pair(0, lw["w_down"])], d, [(x2, "tile", 0)], [(d, F32, "n")], _ep_residual,
                         tm=min(tm, ROW_TILE // 2), tn=COL_TILE_NARROW, name="ffn_down")
    h = rmsnorm_rows(x2, g_ple)
    (x2,) = fused_matmul([h, p2], [_pair(0, lw["w_ple_gate"]), _pair(1, lw["w_ple_proj"])], d,
                         [(x2, "tile", 0)], [(d, F32, "n")], _ep_ple, tm=tm, tn=tn, name="ple")
    return x2


def kernel(x_prompt, x_sample, cache_k, cache_v, state_gla, page_table, p_prompt, p_sample, rel_bias, norm_mix_g, w_in, gla_gate_w2, gla_gate_b, q_norm_g, k_norm_g, lambda_q1, lambda_k1, lambda_q2, lambda_k2, diff_norm_g, gla_norm_g, w_pa, w_pb, w_o, norm_ffn_g, w_gate, w_up, w_down, norm_ple_g, w_ple_gate, w_ple_proj):
    depth = w_in.shape[0]
    bp, lp, d = x_prompt.shape
    bs, ls, _ = x_sample.shape
    assert ls == 1
    _, _, page, h_a, dv_a = cache_v.shape
    dh_a = dv_a // 2
    d_a = h_a * dv_a
    _, _, h_b, dk_b, dv_b = state_gla.shape
    d_b = h_b * dv_b
    rank = gla_gate_w2.shape[1]
    sizes = [d_a, d_a, d_a, h_b * dk_b, h_b * dk_b, d_b, d_b, rank, d, d]
    off_qb, off_kb = 0, h_b * dk_b
    off_vb = 2 * h_b * dk_b
    off_r = off_vb + d_b
    others = {"w_pa": w_pa, "w_pb": w_pb, "w_o": w_o, "w_gate": w_gate, "w_up": w_up, "w_down": w_down,
              "w_ple_gate": w_ple_gate, "w_ple_proj": w_ple_proj}

    mp = bp * lp
    ms = SAMPLE_ROWS
    xp = x_prompt.reshape(mp, d)
    xs = jnp.zeros((ms, d), F32).at[:bs].set(x_sample.reshape(bs, d))
    attn_bias = _prompt_bias_tiles(rel_bias, _tile(lp, ATTN_TILE))
    dec_bias_last, dec_bias_own = _decode_bias(rel_bias, page, h_a)
    kbuf = vbuf = sbuf = None

    sp_l, ks_l, vs_l = [], [], []
    for i in range(depth):
        lam_init = 0.8 - 0.6 * math.exp(-0.3 * i)
        lw = _layer_weights(i, w_in, sizes, others)
        lam_vecs = jnp.stack([lambda_q1[i], lambda_k1[i], lambda_q2[i], lambda_k2[i]]).astype(F32)

        qt, (kbuf, k16, vbuf, vt), rest, gates, log_a = _dense_front(
            xp, lw, norm_mix_g[i], q_norm_g[i], k_norm_g[i], gla_gate_w2[i], gla_gate_b[i], dh_a, ROW_TILE,
            stacks=(i, depth, kbuf, vbuf))
        to3 = lambda t: t.reshape(bp, lp, t.shape[1])
        ya = prompt_attention(qt, to3(k16), vt, attn_bias, lam_vecs, diff_norm_g[i], lam_init, h_a, bp, lp)
        yb, st = prompt_gla(to3(rest), to3(log_a), gla_norm_g[i], h_b, dk_b, dv_b, off_qb, off_kb, off_vb, off_r)
        xp = _dense_back(xp, ya.reshape(mp, d_a), yb.reshape(mp, d_b), gates,
                         p_prompt[i].reshape(mp, -1), lw, norm_ffn_g[i], norm_ple_g[i], ROW_TILE)
        sp_l.append(jnp.swapaxes(st, 2, 3))

        q, (k32, v32), rest, gates, log_a = _dense_front(
            xs, lw, norm_mix_g[i], q_norm_g[i], k_norm_g[i], gla_gate_w2[i], gla_gate_b[i], dh_a, ms)
        ya = decode_attention(i, q[:bs].astype(F32), k32[:bs], v32[:bs], cache_k, cache_v, page_table,
                              dec_bias_last, dec_bias_own, lam_vecs, diff_norm_g[i], lam_init)
        rest32 = rest[:bs].astype(F32)
        hk = lambda t: t.reshape(bs, h_b, -1)
        sbuf, yb = decode_gla(i, state_gla, sbuf, hk(log_a[:bs]),
                              hk(rest32[:, off_kb:off_kb + h_b * dk_b]),
                              hk(rest32[:, off_qb:off_qb + h_b * dk_b]) * (dk_b ** -0.5),
                              hk(rest32[:, off_vb:off_vb + d_b]), hk(rest32[:, off_r:off_r + d_b]),
                              gla_norm_g[i])
        pad = lambda t: jnp.zeros((ms, t.shape[1]), BF16).at[:bs].set(t.astype(BF16))
        ps = jnp.zeros((ms, p_sample.shape[-1]), F32).at[:bs].set(p_sample[i].reshape(bs, -1))
        xs = _dense_back(xs, pad(ya.reshape(bs, d_a)), pad(yb.reshape(bs, d_b)), gates,
                         ps, lw, norm_ffn_g[i], norm_ple_g[i], ms)
        ks_l.append(k32[:bs].reshape(bs, 1, h_a, dv_a))
        vs_l.append(v32[:bs].reshape(bs, 1, h_a, dv_a))

    return (xp.reshape(bp, lp, d), xs[:bs].reshape(bs, 1, d),
            kbuf.reshape(depth, bp, lp, h_a, dv_a), vbuf.reshape(depth, bp, lp, h_a, dv_a),
            jnp.stack(sp_l), jnp.stack(ks_l), jnp.stack(vs_l), sbuf)
```

```python
import functools
import math

import numpy as np
import jax
import jax.numpy as jnp
from jax import lax
from jax.experimental import pallas as pl
from jax.experimental.pallas import tpu as pltpu

EPS = 1e-6
GATE_TAU = 16.0
GLA_CHUNK = 64
MAX_EXACT = 16
MAX_DISTANCE = 128
LOG2E = math.log2(math.e)
NEG = -1e30
VMEM_LIMIT = 56 * 1024 * 1024
ATTN_TILE = 512
GLA_BLOCK = 512
ROW_TILE = 1024
COL_TILE_NARROW = 512
SAMPLE_ROWS = 16
DECODE_PAGES = 8

BF16 = jnp.bfloat16
F32 = jnp.float32


def _cparams(sem):
    return pltpu.CompilerParams(dimension_semantics=sem, vmem_limit_bytes=VMEM_LIMIT)


def _tile(n, pref):
    if n <= pref:
        return n
    t = pref
    while n % t:
        t //= 2
    return t


def _rmsnorm_kernel(x_ref, g_ref, o_ref):
    x = x_ref[...]
    ms = jnp.mean(x * x, axis=-1, keepdims=True)
    o_ref[...] = (x * lax.rsqrt(ms + EPS) * g_ref[...]).astype(o_ref.dtype)


def rmsnorm_rows(x, g):
    m, d = x.shape
    tm = _tile(m, 512)
    return pl.pallas_call(
        _rmsnorm_kernel,
        grid=(m // tm,),
        in_specs=[pl.BlockSpec((tm, d), lambda i: (i, 0)),
                  pl.BlockSpec((1, d), lambda i: (0, 0))],
        out_specs=pl.BlockSpec((tm, d), lambda i: (i, 0)),
        out_shape=jax.ShapeDtypeStruct((m, d), BF16),
        compiler_params=_cparams(("parallel",)),
        name="rmsnorm",
    )(x, g.reshape(1, d).astype(F32))


class _Out:
    def __init__(self, ref, transposed):
        self.ref = ref
        self.transposed = transposed

    def put(self, val, c0=None, c1=None):
        if self.transposed:
            val = val.T.astype(self.ref.dtype)
            if c0 is None:
                self.ref[...] = val
            else:
                self.ref[c0:c1, :] = val
        else:
            val = val.astype(self.ref.dtype)
            if c0 is None:
                self.ref[...] = val
            else:
                self.ref[:, c0:c1] = val


def _mm_kernel(*refs, pair_a, n_a, n_as, n_extra, n_extra_s, out_t, n_out_s, w_f32, epilogue):
    n_w = len(pair_a)
    n_out = len(out_t)
    a_refs = refs[:n_a]
    as_refs = refs[n_a:n_a + n_as]
    pos = n_a + n_as
    w_refs = refs[pos:pos + n_w]
    extras = refs[pos + n_w:pos + n_w + n_extra]
    extras_s = refs[pos + n_w + n_extra:pos + n_w + n_extra + n_extra_s]
    end = len(refs) - sum(w_f32)
    scr = list(refs[end:])
    outs_s = refs[end - n_out_s:end]
    out_refs = refs[end - n_out_s - n_out:end - n_out_s]
    w_use = [scr.pop(0) if f else w for w, f in zip(w_refs, w_f32)]

    @pl.when(pl.program_id(1) == 0)
    def _():
        for w, wb, f in zip(w_refs, w_use, w_f32):
            if f:
                wb[...] = w[...].astype(BF16)
        if n_as:
            s_vals = [a[...].astype(BF16) for a in as_refs]
            accs_s = [jnp.dot(s_vals[ai], wb[...], preferred_element_type=F32)
                      for ai, wb in zip(pair_a, w_use)]
            epilogue(accs_s, extras_s, [_Out(r, False) for r in outs_s])

    a_vals = [a[...].astype(BF16) for a in a_refs]
    accs = [jnp.dot(a_vals[ai], wb[...], preferred_element_type=F32)
            for ai, wb in zip(pair_a, w_use)]
    epilogue(accs, extras, [_Out(r, t) for r, t in zip(out_refs, out_t)])


def _rows_of(a):
    return a[0].shape[1] if isinstance(a, tuple) else a.shape[0]


def _a_spec(a, rows, first_block_only):
    if isinstance(a, tuple):
        arr, layer = a
        if first_block_only:
            return arr, pl.BlockSpec((None, rows, arr.shape[2]), lambda j, i, layer=layer: (layer, 0, 0))
        return arr, pl.BlockSpec((None, rows, arr.shape[2]), lambda j, i, layer=layer: (layer, i, 0))
    if first_block_only:
        return a, pl.BlockSpec((rows, a.shape[1]), lambda j, i: (0, 0))
    return a, pl.BlockSpec((rows, a.shape[1]), lambda j, i: (i, 0))


def fused_matmul(a_list, pairs, n, extras, outs, epilogue, *, tm, tn=1024, stacked=None, small=None,
                 name="mm"):
    m = _rows_of(a_list[0])
    tm = _tile(m, tm)
    tn = _tile(n, tn)
    grid = (n // tn, m // tm)
    a_list_s, extras_s, outs_s = small if small is not None else ([], [], [])
    ms = _rows_of(a_list_s[0]) if a_list_s else 0
    in_specs, args, scratch, w_f32 = [], [], [], []
    for a in a_list:
        arr, spec = _a_spec(a, tm, False)
        in_specs.append(spec)
        args.append(arr)
    for a in a_list_s:
        arr, spec = _a_spec(a, ms, True)
        in_specs.append(spec)
        args.append(arr)
    for _, w, layer, col0 in pairs:
        assert col0 % tn == 0
        k_dim = w.shape[-2]
        if w.ndim == 3:
            in_specs.append(pl.BlockSpec((None, k_dim, tn),
                                         lambda j, i, layer=layer, c=col0 // tn: (layer, 0, c + j)))
        else:
            in_specs.append(pl.BlockSpec((k_dim, tn), lambda j, i, c=col0 // tn: (0, c + j)))
        args.append(w)
        w_f32.append(w.dtype != BF16)
        if w_f32[-1]:
            scratch.append(pltpu.VMEM((k_dim, tn), BF16))
    for arr, kind, off in extras:
        if kind == "tile":
            in_specs.append(pl.BlockSpec((tm, tn), lambda j, i, off=off: (i, off + j)))
        else:
            in_specs.append(pl.BlockSpec(arr.shape, lambda j, i, nd=arr.ndim: (0,) * nd))
        args.append(arr)
    for arr, kind, off in extras_s:
        if kind == "tile":
            in_specs.append(pl.BlockSpec((ms, tn), lambda j, i, off=off: (0, off + j)))
        else:
            in_specs.append(pl.BlockSpec(arr.shape, lambda j, i, nd=arr.ndim: (0,) * nd))
        args.append(arr)
    out_specs, out_shape, aliases = [], [], {}
    for oi, (width, dt, kind) in enumerate(outs):
        if stacked is not None and stacked[0] == oi:
            _, layer, depth, buf = stacked
            assert kind == "n"
            out_specs.append(pl.BlockSpec((None, tm, tn), lambda j, i, layer=layer: (layer, i, j)))
            out_shape.append(jax.ShapeDtypeStruct((depth, m, n), dt))
            if buf is not None:
                assert buf.shape == (depth, m, n) and buf.dtype == dt
                in_specs.append(pl.BlockSpec(memory_space=pl.ANY))
                aliases[len(args)] = oi
                args.append(buf)
        elif kind == "n":
            out_specs.append(pl.BlockSpec((tm, tn), lambda j, i: (i, j)))
            out_shape.append(jax.ShapeDtypeStruct((m, n), dt))
        elif kind == "t":
            out_specs.append(pl.BlockSpec((tn, tm), lambda j, i: (j, i)))
            out_shape.append(jax.ShapeDtypeStruct((n, m), dt))
        else:
            assert grid[0] == 1
            out_specs.append(pl.BlockSpec((tm, width), lambda j, i: (i, 0)))
            out_shape.append(jax.ShapeDtypeStruct((m, width), dt))
    for width, dt, kind in outs_s:
        if kind == "n":
            out_specs.append(pl.BlockSpec((ms, tn), lambda j, i: (0, j)))
            out_shape.append(jax.ShapeDtypeStruct((ms, n), dt))
        else:
            assert kind == "whole" and grid[0] == 1
            out_specs.append(pl.BlockSpec((ms, width), lambda j, i: (0, 0)))
            out_shape.append(jax.ShapeDtypeStruct((ms, width), dt))
    kern = functools.partial(_mm_kernel, pair_a=tuple(p[0] for p in pairs), n_a=len(a_list),
                             n_as=len(a_list_s), n_extra=len(extras), n_extra_s=len(extras_s),
                             out_t=tuple(o[2] == "t" for o in outs), n_out_s=len(outs_s),
                             w_f32=tuple(w_f32), epilogue=epilogue)
    return pl.pallas_call(
        kern, grid=grid, in_specs=in_specs, out_specs=out_specs, out_shape=out_shape,
        scratch_shapes=scratch, input_output_aliases=aliases,
        compiler_params=_cparams(("parallel", "arbitrary")), name=name,
    )(*args)


def _ep_cast(accs, extras, outs):
    for o in outs:
        o.put(accs[0])


def _ep_groupnorm(accs, extras, outs, *, group, scale):
    acc = accs[0]
    g = extras[0][...] * scale
    for s in range(acc.shape[1] // group):
        z = acc[:, s * group:(s + 1) * group]
        ms = jnp.mean(z * z, axis=-1, keepdims=True)
        y = z * lax.rsqrt(ms + EPS) * g
        for o in outs:
            o.put(y, s * group, (s + 1) * group)


def _ep_gate(accs, extras, outs):
    y = jnp.dot(accs[0].astype(BF16), extras[0][...].astype(BF16),
                preferred_element_type=F32) + extras[1][...]
    ls = jnp.minimum(y, 0.0) - jnp.log1p(jnp.exp(-jnp.abs(y)))
    outs[0].put(ls * (1.0 / GATE_TAU))


def _ep_merge(accs, extras, outs):
    ga = extras[0][...].astype(F32)
    gb = extras[1][...].astype(F32)
    outs[0].put(jax.nn.sigmoid(ga) * accs[0] + jax.nn.sigmoid(gb) * accs[1])


def _ep_residual(accs, extras, outs):
    outs[0].put(extras[0][...] + accs[0])


def _ep_swiglu(accs, extras, outs):
    g = accs[0]
    outs[0].put(g * jax.nn.sigmoid(g) * accs[1])


def _ep_ple(accs, extras, outs):
    outs[0].put(extras[0][...] + jax.nn.sigmoid(accs[0]) * accs[1])


def _bucket_table(n, n_buckets):
    d = np.arange(n)
    nf = np.maximum(d, 1).astype(np.float64)
    large = MAX_EXACT + (np.log(nf / MAX_EXACT) / math.log(MAX_DISTANCE / MAX_EXACT)
                         * (n_buckets - MAX_EXACT)).astype(np.int64)
    large = np.minimum(large, n_buckets - 1)
    return np.where(d < MAX_EXACT, d, large).astype(np.int32)


def _toeplitz(w, t):
    n = 2 * t
    big = jnp.tile(w, (1, t))[:, :t * (n - 1)].reshape(w.shape[0], t, n - 1)
    return big[:, :, :t]


def _prompt_bias_tiles(rel_bias, t):
    n_buckets = rel_bias.shape[0]
    assert t + 1 >= MAX_DISTANCE
    table = _bucket_table(2 * t, n_buckets)
    rb = rel_bias.astype(F32)
    f = ((rb[table] - rb[n_buckets - 1][None, :]) * LOG2E).T
    neg = jnp.full((f.shape[0], t), NEG, F32)
    diag = _toeplitz(jnp.concatenate([f[:, :t], neg], axis=1), t)
    sub = _toeplitz(jnp.concatenate([f[:, t:], f[:, :t]], axis=1), t)
    return jnp.stack([diag, sub], axis=1)


def _decode_bias(rel_bias, page, n_heads):
    n_buckets = rel_bias.shape[0]
    assert page + 1 >= MAX_DISTANCE
    table = _bucket_table(page + 1, n_buckets)
    rel = (rel_bias.astype(F32) - rel_bias.astype(F32)[n_buckets - 1][None, :]) * LOG2E
    head_of_row = np.arange(2 * n_heads) % n_heads
    by_dist = rel[table][:, head_of_row].T
    last = jnp.repeat(by_dist[:, page - np.arange(page)], n_heads, axis=1)
    own = by_dist[:, 0:1]
    return last, own


def _attn_kernel(q_ref, k_ref, v_ref, bias_ref, lam_ref, g_ref, o_ref, m_sc, l_sc, acc_sc,
                 *, lam_init, dh, t):
    qi = pl.program_id(2)
    m_sc[...] = jnp.full(m_sc.shape, NEG, F32)
    l_sc[...] = jnp.zeros(l_sc.shape, F32)
    acc_sc[...] = jnp.zeros(acc_sc.shape, F32)

    def tile(j, bias):
        ks = pl.ds(pl.multiple_of(j * t, t), t)
        vt = v_ref[:, ks]
        scores = [jnp.dot(k_ref[0, ks, c * dh:(c + 1) * dh],
                          q_ref[c * dh:(c + 1) * dh, :],
                          preferred_element_type=F32) for c in range(2)]
        probs, alphas = [], []
        for c in range(2):
            s = scores[c]
            if bias is not None:
                s = s + bias
            m_prev = m_sc[c]
            m_new = jnp.maximum(m_prev, jnp.max(s, axis=0, keepdims=True))
            alpha = jnp.exp2(m_prev - m_new)
            pr = jnp.exp2(s - m_new)
            l_sc[c] = alpha * l_sc[c] + jnp.sum(pr, axis=0, keepdims=True)
            m_sc[c] = m_new
            probs.append(pr.astype(BF16))
            alphas.append(alpha)
        for c in range(2):
            acc_sc[c] = alphas[c] * acc_sc[c] + jnp.dot(vt, probs[c], preferred_element_type=F32)

    def far(j, carry):
        tile(j, None)
        return carry

    lax.fori_loop(0, qi - 1, far, 0)

    @pl.when(qi >= 1)
    def _():
        tile(qi - 1, bias_ref[0, 1])

    tile(qi, bias_ref[0, 0])

    lv = lam_ref[...]
    lam = (jnp.exp(jnp.sum(lv[0:1] * lv[1:2], axis=-1, keepdims=True))
           - jnp.exp(jnp.sum(lv[2:3] * lv[3:4], axis=-1, keepdims=True)) + lam_init)
    r0 = 1.0 / l_sc[0]
    r1 = lam / l_sc[1]
    ot = acc_sc[0] * r0 - acc_sc[1] * r1
    ms = jnp.mean(ot * ot, axis=0, keepdims=True)
    yt = ot * lax.rsqrt(ms + EPS) * (g_ref[...] * (1.0 - lam_init))
    o_ref[0] = yt.T.astype(o_ref.dtype)


def prompt_attention(qt, k, vt, bias, lam_vecs, diff_g, lam_init, n_heads, b, l):
    dtot = k.shape[2]
    dv = dtot // n_heads
    dh = dv // 2
    t = bias.shape[-1]
    nq = l // t
    kern = functools.partial(_attn_kernel, lam_init=lam_init, dh=dh, t=t)
    return pl.pallas_call(
        kern,
        grid=(b, n_heads, nq),
        in_specs=[
            pl.BlockSpec((dv, t), lambda bi, h, qi: (h, bi * nq + qi)),
            pl.BlockSpec((1, l, dv), lambda bi, h, qi: (bi, 0, h)),
            pl.BlockSpec((dv, l), lambda bi, h, qi: (h, bi)),
            pl.BlockSpec((1, 2, t, t), lambda bi, h, qi: (h, 0, 0, 0)),
            pl.BlockSpec((4, dh), lambda bi, h, qi: (0, 0)),
            pl.BlockSpec((dv, 1), lambda bi, h, qi: (0, 0)),
        ],
        out_specs=pl.BlockSpec((1, t, dv), lambda bi, h, qi: (bi, qi, h)),
        out_shape=jax.ShapeDtypeStruct((b, l, dtot), BF16),
        scratch_shapes=[pltpu.VMEM((2, 1, t), F32), pltpu.VMEM((2, 1, t), F32),
                        pltpu.VMEM((2, dv, t), F32)],
        compiler_params=_cparams(("parallel", "parallel", "arbitrary")),
        name="diff_attn_prompt",
    )(qt, k, vt, bias, lam_vecs, diff_g.reshape(dv, 1).astype(F32))


def _gla_kernel(q_ref, k_ref, v_ref, r_ref, g_ref, gn_ref, y_ref, s_ref, st_sc, *, chunk, nchunk, qscale):
    li = pl.program_id(1)
    nb = q_ref.shape[0]

    @pl.when(li == 0)
    def _():
        st_sc[...] = jnp.zeros(st_sc.shape, F32)

    lb = nchunk * chunk
    row = lax.broadcasted_iota(jnp.int32, (chunk, chunk), 0)
    col = lax.broadcasted_iota(jnp.int32, (chunk, chunk), 1)
    tril = (row >= col).astype(BF16)
    brow = lax.broadcasted_iota(jnp.int32, (lb, lb), 0)
    bcol = lax.broadcasted_iota(jnp.int32, (lb, lb), 1)
    block_causal = (brow >= bcol) & (brow // chunk == bcol // chunk)
    gn = gn_ref[...]
    nt = (((1,), (1,)), ((), ()))
    tn = (((0,), (0,)), ((), ()))

    pre = []
    for b in range(nb):
        g = g_ref[b]
        g_hi = g.astype(BF16)
        r1 = g - g_hi.astype(F32)
        g_mid = r1.astype(BF16)
        g_lo = (r1 - g_mid.astype(F32)).astype(BF16)
        cums, tots = [], []
        for c in range(nchunk):
            sl = slice(c * chunk, (c + 1) * chunk)
            bc = (jnp.dot(tril, g_hi[sl], preferred_element_type=F32)
                  + jnp.dot(tril, g_mid[sl], preferred_element_type=F32)
                  + jnp.dot(tril, g_lo[sl], preferred_element_type=F32))
            cums.append(bc)
            tots.append(jnp.broadcast_to(bc[chunk - 1:chunk, :], bc.shape))
        bcum = jnp.concatenate(cums, axis=0)
        btot = jnp.concatenate(tots, axis=0)
        q = q_ref[b].astype(F32) * qscale
        k = k_ref[b].astype(F32)
        v = v_ref[b]
        qt = (q * jnp.exp(bcum)).astype(BF16)
        kt = (k * jnp.exp(-bcum)).astype(BF16)
        k2 = (k * jnp.exp(btot - bcum)).astype(BF16)
        a = lax.dot_general(qt, kt, nt, preferred_element_type=F32)
        a = jnp.where(block_causal, a, 0.0)
        o_intra = jnp.dot(a.astype(BF16), v, preferred_element_type=F32)
        pre.append((qt, k2, v, o_intra, jnp.exp(btot)))

    states = [st_sc[b] for b in range(nb)]
    outs = [[] for _ in range(nb)]
    for c in range(nchunk):
        sl = slice(c * chunk, (c + 1) * chunk)
        for b in range(nb):
            qt, k2, v, o_intra, decay = pre[b]
            st = states[b]
            outs[b].append(o_intra[sl] + lax.dot_general(qt[sl], st.astype(BF16), nt,
                                                         preferred_element_type=F32))
            states[b] = st * decay[c * chunk:c * chunk + 1, :] + lax.dot_general(
                v[sl], k2[sl], tn, preferred_element_type=F32)

    for b in range(nb):
        st_sc[b] = states[b]
        o = jnp.concatenate(outs[b], axis=0)
        ms = jnp.mean(o * o, axis=-1, keepdims=True)
        rr = r_ref[b].astype(F32)
        y = o * lax.rsqrt(ms + EPS) * gn * (rr * jax.nn.sigmoid(rr))
        y_ref[b] = y.astype(y_ref.dtype)

    @pl.when(li == pl.num_programs(1) - 1)
    def _():
        s_ref[:, 0] = st_sc[...]


def prompt_gla(z, log_a, gla_g, n_heads, dk, dv, off_q, off_k, off_v, off_r):
    b, l, _ = z.shape
    chunk = math.gcd(l, GLA_CHUNK)
    lb = _tile(l, GLA_BLOCK)
    nchunk = lb // chunk
    kern = functools.partial(_gla_kernel, chunk=chunk, nchunk=nchunk, qscale=dk ** -0.5)
    oq, ok, ov, orr = off_q // dk, off_k // dk, off_v // dv, off_r // dv
    return pl.pallas_call(
        kern,
        grid=(n_heads, l // lb),
        in_specs=[
            pl.BlockSpec((b, lb, dk), lambda h, i: (0, i, oq + h)),
            pl.BlockSpec((b, lb, dk), lambda h, i: (0, i, ok + h)),
            pl.BlockSpec((b, lb, dv), lambda h, i: (0, i, ov + h)),
            pl.BlockSpec((b, lb, dv), lambda h, i: (0, i, orr + h)),
            pl.BlockSpec((b, lb, dk), lambda h, i: (0, i, h)),
            pl.BlockSpec((1, dv), lambda h, i: (0, 0)),
        ],
        out_specs=[
            pl.BlockSpec((b, lb, dv), lambda h, i: (0, i, h)),
            pl.BlockSpec((b, 1, dv, dk), lambda h, i: (0, h, 0, 0)),
        ],
        out_shape=[jax.ShapeDtypeStruct((b, l, n_heads * dv), BF16),
                   jax.ShapeDtypeStruct((b, n_heads, dv, dk), F32)],
        scratch_shapes=[pltpu.VMEM((b, dv, dk), F32)],
        compiler_params=_cparams(("parallel", "arbitrary")),
        name="gla_prompt",
    )(z, z, z, z, log_a, gla_g.reshape(1, dv).astype(F32))


def _decode_attn_kernel(pt_ref, q_ref, *refs, n_sub, n_heads, lam_init):
    k_refs = refs[:n_sub]
    v_refs = refs[n_sub:2 * n_sub]
    (bias_last_ref, own_ref, kn_ref, vn_ref, lam_ref, g_ref, o_ref, m_sc, l_sc, acc_sc) = refs[2 * n_sub:]
    s_idx = pl.program_id(1)
    last = pl.num_programs(1) - 1

    @pl.when(s_idx == 0)
    def _():
        m_sc[...] = jnp.full(m_sc.shape, NEG, F32)
        l_sc[...] = jnp.zeros(l_sc.shape, F32)
        acc_sc[...] = jnp.zeros(acc_sc.shape, F32)

    q = q_ref[0]
    nrow = q.shape[0]

    def update(kmats, vmats, biases):
        parts = []
        for kmat, bias in zip(kmats, biases):
            sp = lax.dot_general(q, kmat, (((1,), (1,)), ((), ())), preferred_element_type=F32)
            parts.append(sp if bias is None else sp + bias)
        s = parts[0] if len(parts) == 1 else jnp.concatenate(parts, axis=1)
        r = s.shape[1]
        rowh = lax.broadcasted_iota(jnp.int32, (nrow, r), 0) % n_heads
        colh = lax.broadcasted_iota(jnp.int32, (nrow, r), 1) % n_heads
        s = jnp.where(rowh == colh, s, NEG)
        m_prev = m_sc[...]
        m_new = jnp.maximum(m_prev, jnp.max(s, axis=-1, keepdims=True))
        alpha = jnp.exp2(m_prev - m_new)
        pr = jnp.exp2(s - m_new).astype(BF16)
        l_sc[...] = alpha * l_sc[...] + jnp.sum(pr.astype(F32), axis=-1, keepdims=True)
        pv = None
        off = 0
        for vmat in vmats:
            d = jnp.dot(pr[:, off:off + vmat.shape[0]], vmat, preferred_element_type=F32)
            pv = d if pv is None else pv + d
            off += vmat.shape[0]
        acc_sc[...] = alpha * acc_sc[...] + pv
        m_sc[...] = m_new

    kmats, vmats = [], []
    for u in range(n_sub):
        kp = k_refs[u][...]
        vp = v_refs[u][...]
        kmats.append(kp.reshape(kp.shape[0] * kp.shape[1], kp.shape[2]).astype(BF16))
        vmats.append(vp.reshape(vp.shape[0] * vp.shape[1], vp.shape[2]).astype(BF16))
    is_last = (s_idx == last).astype(F32)
    update(kmats, vmats, [None] * (n_sub - 1) + [bias_last_ref[...] * is_last])

    @pl.when(s_idx == last)
    def _():
        update([kn_ref[0]], [vn_ref[0]], [own_ref[...]])
        lv = lam_ref[...]
        lam = (jnp.exp(jnp.sum(lv[0:1] * lv[1:2], axis=-1, keepdims=True))
               - jnp.exp(jnp.sum(lv[2:3] * lv[3:4], axis=-1, keepdims=True)) + lam_init)
        on = acc_sc[...] / l_sc[...]
        o = on[:n_heads] - lam * on[n_heads:]
        ms = jnp.mean(o * o, axis=-1, keepdims=True)
        o_ref[0] = o * lax.rsqrt(ms + EPS) * g_ref[...] * (1.0 - lam_init)


def decode_attention(layer, q, k_new, v_new, cache_k, cache_v, page_table, bias_last, bias_own, lam_vecs,
                     diff_g, lam_init):
    nb, n_pages = page_table.shape
    _, _, page, n_heads, dv = cache_v.shape
    dh = dv // 2
    n_sub = DECODE_PAGES if n_pages % DECODE_PAGES == 0 else 1
    n_steps = n_pages // n_sub
    rows = page * n_heads
    qh = q.reshape(nb, n_heads, 2 * dh)
    half = (np.arange(2 * dh) // dh)[None, :] == np.arange(2)[:, None]
    qall = (qh[:, None, :, :] * jnp.asarray(half, F32)[None, :, None, :]).reshape(nb, 2 * n_heads, 2 * dh)
    qall = qall.astype(BF16)
    kn = k_new.reshape(nb, n_heads, 2 * dh).astype(BF16)
    vn = v_new.reshape(nb, n_heads, dv).astype(BF16)
    pt_flat = page_table.reshape(-1).astype(jnp.int32)

    def page_spec(u):
        return pl.BlockSpec((None, None, page, n_heads, dv),
                            lambda b, s, pt, u=u: (layer, pt[b * n_pages + s * n_sub + u], 0, 0, 0))

    const2 = lambda b, s, pt: (0, 0)
    grid_spec = pltpu.PrefetchScalarGridSpec(
        num_scalar_prefetch=1,
        grid=(nb, n_steps),
        in_specs=[pl.BlockSpec((1, 2 * n_heads, 2 * dh), lambda b, s, pt: (b, 0, 0))]
        + [page_spec(u) for u in range(n_sub)] + [page_spec(u) for u in range(n_sub)]
        + [pl.BlockSpec((2 * n_heads, rows), const2),
           pl.BlockSpec((2 * n_heads, 1), const2),
           pl.BlockSpec((1, n_heads, 2 * dh), lambda b, s, pt: (b, 0, 0)),
           pl.BlockSpec((1, n_heads, dv), lambda b, s, pt: (b, 0, 0)),
           pl.BlockSpec((4, dh), const2),
           pl.BlockSpec((1, dv), const2)],
        out_specs=pl.BlockSpec((1, n_heads, dv), lambda b, s, pt: (b, 0, 0)),
        scratch_shapes=[pltpu.VMEM((2 * n_heads, 1), F32), pltpu.VMEM((2 * n_heads, 1), F32),
                        pltpu.VMEM((2 * n_heads, dv), F32)],
    )
    kern = functools.partial(_decode_attn_kernel, n_sub=n_sub, n_heads=n_heads, lam_init=lam_init)
    return pl.pallas_call(
        kern, grid_spec=grid_spec,
        out_shape=jax.ShapeDtypeStruct((nb, n_heads, dv), F32),
        compiler_params=_cparams(("parallel", "arbitrary")),
        name="diff_attn_decode",
    )(pt_flat, qall, *([cache_k] * n_sub), *([cache_v] * n_sub), bias_last, bias_own, kn, vn,
      lam_vecs, diff_g.reshape(1, dv).astype(F32))


def _gla_step_kernel(s_ref, a_ref, k_ref, q_ref, v_ref, r_ref, gn_ref, *rest):
    s_out_ref, y_ref = rest[-2:]
    s_new = jnp.exp(a_ref[0, 0]) * s_ref[...] + k_ref[0, 0] * v_ref[0, 0]
    s_out_ref[...] = s_new
    o = jnp.sum(q_ref[0, 0] * s_new, axis=0, keepdims=True)
    ms = jnp.mean(o * o, axis=-1, keepdims=True)
    rr = r_ref[0, 0]
    y_ref[0, 0] = o * lax.rsqrt(ms + EPS) * gn_ref[...] * (rr * jax.nn.sigmoid(rr))


def decode_gla(layer, state, new_states, log_a, k, q, v, r, gla_g):
    _, nb, n_heads, dk, dv = state.shape
    col = lambda t: t.reshape(nb, n_heads, dk, 1)
    rowv = lambda t: t.reshape(nb, n_heads, 1, dv)
    cspec = pl.BlockSpec((1, 1, dk, 1), lambda b, h: (b, h, 0, 0))
    rspec = pl.BlockSpec((1, 1, 1, dv), lambda b, h: (b, h, 0, 0))
    sspec = pl.BlockSpec((None, None, None, dk, dv), lambda b, h: (layer, b, h, 0, 0))
    in_specs = [sspec, cspec, cspec, cspec, rspec, rspec, pl.BlockSpec((1, dv), lambda b, h: (0, 0))]
    args = [state, col(log_a), col(k), col(q), rowv(v), rowv(r), gla_g.reshape(1, dv).astype(F32)]
    aliases = {}
    if new_states is not None:
        in_specs.append(pl.BlockSpec(memory_space=pl.ANY))
        aliases[len(args)] = 0
        args.append(new_states)
    return pl.pallas_call(
        _gla_step_kernel,
        grid=(nb, n_heads),
        in_specs=in_specs,
        out_specs=[sspec, rspec],
        out_shape=[jax.ShapeDtypeStruct(state.shape, F32),
                   jax.ShapeDtypeStruct((nb, n_heads, 1, dv), F32)],
        input_output_aliases=aliases,
        compiler_params=_cparams(("parallel", "parallel")),
        name="gla_decode",
    )(*args)


def _split_w_in(w_in, sizes):
    offs = [int(o) for o in np.concatenate([[0], np.cumsum(sizes)])]
    depth = w_in.shape[0]
    main = [w_in[i, :, :offs[7]].astype(BF16) for i in range(depth)]
    gates = [w_in[i, :, offs[8]:offs[10]].astype(BF16) for i in range(depth)]
    glow = [w_in[i, :, offs[7]:offs[8]].astype(BF16) for i in range(depth)]
    return offs, main, gates, glow


def _layer_weights(i, split, sizes, others):
    offs, main, gates, glow = split
    w = {"q": (main[i], None, offs[0], sizes[0]), "k": (main[i], None, offs[1], sizes[1]),
         "v": (main[i], None, offs[2], sizes[2]), "rest": (main[i], None, offs[3], offs[7] - offs[3]),
         "gates": (gates[i], None, 0, offs[10] - offs[8]), "glow": (glow[i], None, 0, sizes[7])}
    for name, arr in others.items():
        w[name] = (arr, i, 0, arr.shape[2])
    return w


def _pair(a_idx, wspec):
    return (a_idx, wspec[0], wspec[1], wspec[2])


def _dense_front(xp, xs, lw, g_mix, q_g, k_g, gate_w2, gate_b, dh, stacks):
    hp = rmsnorm_rows(xp, g_mix)
    hs = rmsnorm_rows(xs, g_mix)
    layer, depth, kbuf, vbuf = stacks
    d_qk = lw["q"][3]
    d_v = lw["v"][3]
    qn = functools.partial(_ep_groupnorm, group=dh, scale=dh ** -0.5 * LOG2E)
    kn = functools.partial(_ep_groupnorm, group=dh, scale=1.0)
    gq = (q_g.reshape(1, dh).astype(F32), "full", 0)
    gk = (k_g.reshape(1, dh).astype(F32), "full", 0)
    tm = ROW_TILE
    qt, q_s = fused_matmul([hp], [_pair(0, lw["q"])], d_qk, [gq], [(d_qk, BF16, "t")], qn, tm=tm,
                           small=([hs], [gq], [(d_qk, BF16, "n")]), name="in_q")
    kbuf, k16, k_s = fused_matmul([hp], [_pair(0, lw["k"])], d_qk, [gk], [(d_qk, F32, "n"), (d_qk, BF16, "n")],
                                  kn, tm=tm, stacked=(0, layer, depth, kbuf),
                                  small=([hs], [gk], [(d_qk, F32, "n")]), name="in_k")
    vbuf, vt, v_s = fused_matmul([hp], [_pair(0, lw["v"])], d_v, [], [(d_v, F32, "n"), (d_v, BF16, "t")],
                                 _ep_cast, tm=tm, stacked=(0, layer, depth, vbuf),
                                 small=([hs], [], [(d_v, F32, "n")]), name="in_v")
    n_rest = lw["rest"][3]
    rest, rest_s = fused_matmul([hp], [_pair(0, lw["rest"])], n_rest, [], [(n_rest, BF16, "n")], _ep_cast,
                                tm=tm, small=([hs], [], [(n_rest, BF16, "n")]), name="in_rest")
    n_gates = lw["gates"][3]
    gates, gates_s = fused_matmul([hp], [_pair(0, lw["gates"])], n_gates, [], [(n_gates, BF16, "n")], _ep_cast,
                                  tm=tm, small=([hs], [], [(n_gates, BF16, "n")]), name="in_gates")
    n_gate = gate_w2.shape[1]
    gate_extras = [(gate_w2.astype(F32), "full", 0), (gate_b.reshape(1, n_gate).astype(F32), "full", 0)]
    log_a, log_a_s = fused_matmul([hp], [_pair(0, lw["glow"])], lw["glow"][3], gate_extras,
                                  [(n_gate, F32, "whole")], _ep_gate, tm=tm,
                                  small=([hs], gate_extras, [(n_gate, F32, "whole")]), name="in_gate")
    return ((qt, kbuf, k16, vbuf, vt, rest, gates, log_a), (q_s, k_s, v_s, rest_s, gates_s, log_a_s))


def _dense_back(xp, xs, y_p, y_s, gates_p, gates_s, p_p, p_s, lw, g_ffn, g_ple):
    d = xp.shape[1]
    tm = ROW_TILE
    tn = _tile(d, COL_TILE_NARROW)
    gate_tiles = lambda g: [(g, "tile", 0), (g, "tile", d // tn)]
    merged, merged_s = fused_matmul(list(y_p), [_pair(0, lw["w_pa"]), _pair(1, lw["w_pb"])], d,
                                    gate_tiles(gates_p), [(d, BF16, "n")], _ep_merge, tm=tm, tn=tn,
                                    small=(list(y_s), gate_tiles(gates_s), [(d, BF16, "n")]), name="merge")
    xp, xs = fused_matmul([merged], [_pair(0, lw["w_o"])], d, [(xp, "tile", 0)], [(d, F32, "n")],
                          _ep_residual, tm=tm, tn=tn,
                          small=([merged_s], [(xs, "tile", 0)], [(d, F32, "n")]), name="out_proj")
    hp = rmsnorm_rows(xp, g_ffn)
    hs = rmsnorm_rows(xs, g_ffn)
    d_ff = lw["w_gate"][3]
    gu, gu_s = fused_matmul([hp], [_pair(0, lw["w_gate"]), _pair(0, lw["w_up"])], d_ff, [], [(d_ff, BF16, "n")],
                            _ep_swiglu, tm=tm, tn=COL_TILE_NARROW,
                            small=([hs], [], [(d_ff, BF16, "n")]), name="ffn_up")
    xp, xs = fused_matmul([gu], [_pair(0, lw["w_down"])], d, [(xp, "tile", 0)], [(d, F32, "n")], _ep_residual,
                          tm=tm // 2, tn=COL_TILE_NARROW,
                          small=([gu_s], [(xs, "tile", 0)], [(d, F32, "n")]), name="ffn_down")
    hp = rmsnorm_rows(xp, g_ple)
    hs = rmsnorm_rows(xs, g_ple)
    xp, xs = fused_matmul([hp, p_p], [_pair(0, lw["w_ple_gate"]), _pair(1, lw["w_ple_proj"])], d,
                          [(xp, "tile", 0)], [(d, F32, "n")], _ep_ple, tm=tm, tn=tn,
                          small=([hs, p_s], [(xs, "tile", 0)], [(d, F32, "n")]), name="ple")
    return xp, xs


def kernel(x_prompt, x_sample, cache_k, cache_v, state_gla, page_table, p_prompt, p_sample, rel_bias, norm_mix_g, w_in, gla_gate_w2, gla_gate_b, q_norm_g, k_norm_g, lambda_q1, lambda_k1, lambda_q2, lambda_k2, diff_norm_g, gla_norm_g, w_pa, w_pb, w_o, norm_ffn_g, w_gate, w_up, w_down, norm_ple_g, w_ple_gate, w_ple_proj):
    depth = w_in.shape[0]
    bp, lp, d = x_prompt.shape
    bs, ls, _ = x_sample.shape
    assert ls == 1
    _, _, page, h_a, dv_a = cache_v.shape
    dh_a = dv_a // 2
    d_a = h_a * dv_a
    _, _, h_b, dk_b, dv_b = state_gla.shape
    d_b = h_b * dv_b
    rank = gla_gate_w2.shape[1]
    sizes = [d_a, d_a, d_a, h_b * dk_b, h_b * dk_b, d_b, d_b, rank, d, d]
    off_qb, off_kb = 0, h_b * dk_b
    off_vb = 2 * h_b * dk_b
    off_r = off_vb + d_b
    others = {"w_pa": w_pa, "w_pb": w_pb, "w_o": w_o, "w_gate": w_gate, "w_up": w_up, "w_down": w_down,
              "w_ple_gate": w_ple_gate, "w_ple_proj": w_ple_proj}

    mp = bp * lp
    ms = SAMPLE_ROWS
    xp = x_prompt.reshape(mp, d)
    xs = jnp.zeros((ms, d), F32).at[:bs].set(x_sample.reshape(bs, d))
    attn_bias = _prompt_bias_tiles(rel_bias, _tile(lp, ATTN_TILE))
    dec_bias_last, dec_bias_own = _decode_bias(rel_bias, page, h_a)
    split = _split_w_in(w_in, sizes)
    p_prompt3 = p_prompt.reshape(depth, mp, -1)
    p_sample3 = jnp.zeros((depth, ms, p_sample.shape[-1]), F32).at[:, :bs].set(p_sample.reshape(depth, bs, -1))
    kbuf = vbuf = sbuf = None

    sp_l, ks_l, vs_l = [], [], []
    for i in range(depth):
        lam_init = 0.8 - 0.6 * math.exp(-0.3 * i)
        lw = _layer_weights(i, split, sizes, others)
        lam_vecs = jnp.stack([lambda_q1[i], lambda_k1[i], lambda_q2[i], lambda_k2[i]]).astype(F32)

        (qt, kbuf, k16, vbuf, vt, rest, gates, log_a), (q_s, k_s, v_s, rest_s, gates_s, log_a_s) = _dense_front(
            xp, xs, lw, norm_mix_g[i], q_norm_g[i], k_norm_g[i], gla_gate_w2[i], gla_gate_b[i], dh_a,
            (i, depth, kbuf, vbuf))

        to3 = lambda t: t.reshape(bp, lp, t.shape[1])
        ya = prompt_attention(qt, to3(k16), vt, attn_bias, lam_vecs, diff_norm_g[i], lam_init, h_a, bp, lp)
        yb, st = prompt_gla(to3(rest), to3(log_a), gla_norm_g[i], h_b, dk_b, dv_b, off_qb, off_kb, off_vb, off_r)
        sp_l.append(jnp.swapaxes(st, 2, 3))

        ya_s = decode_attention(i, q_s[:bs].astype(F32), k_s[:bs], v_s[:bs], cache_k, cache_v, page_table,
                                dec_bias_last, dec_bias_own, lam_vecs, diff_norm_g[i], lam_init)
        rest32 = rest_s[:bs].astype(F32)
        hk = lambda t: t.reshape(bs, h_b, -1)
        sbuf, yb_s = decode_gla(i, state_gla, sbuf, hk(log_a_s[:bs]),
                                hk(rest32[:, off_kb:off_kb + h_b * dk_b]),
                                hk(rest32[:, off_qb:off_qb + h_b * dk_b]) * (dk_b ** -0.5),
                                hk(rest32[:, off_vb:off_vb + d_b]), hk(rest32[:, off_r:off_r + d_b]),
                                gla_norm_g[i])
        pad = lambda t: jnp.zeros((ms, t.shape[1]), BF16).at[:bs].set(t.astype(BF16))

        xp, xs = _dense_back(xp, xs, (ya.reshape(mp, d_a), yb.reshape(mp, d_b)),
                             (pad(ya_s.reshape(bs, d_a)), pad(yb_s.reshape(bs, d_b))), gates, gates_s,
                             (p_prompt3, i), (p_sample3, i), lw, norm_ffn_g[i], norm_ple_g[i])
        ks_l.append(k_s[:bs].reshape(bs, 1, h_a, dv_a))
        vs_l.append(v_s[:bs].reshape(bs, 1, h_a, dv_a))

    return (xp.reshape(bp, lp, d), xs[:bs].reshape(bs, 1, d),
            kbuf.reshape(depth, bp, lp, h_a, dv_a), vbuf.reshape(depth, bp, lp, h_a, dv_a),
            jnp.stack(sp_l), jnp.stack(ks_l), jnp.stack(vs_l), sbuf)
```

```python
import functools
import math

import numpy as np
import jax
import jax.numpy as jnp
from jax import lax
from jax.experimental import pallas as pl
from jax.experimental.pallas import tpu as pltpu

EPS = 1e-6
GATE_TAU = 16.0
GLA_CHUNK = 64
MAX_EXACT = 16
MAX_DISTANCE = 128
LOG2E = math.log2(math.e)
NEG = -1e30
VMEM_LIMIT = 56 * 1024 * 1024
ATTN_TILE = 512
ATTN_HEADS = 2
GLA_BLOCK = 512
ROW_TILE = 1024
COL_TILE_NARROW = 512
SAMPLE_ROWS = 16
DECODE_PAGES = 8

BF16 = jnp.bfloat16
F32 = jnp.float32


def _cparams(sem):
    return pltpu.CompilerParams(dimension_semantics=sem, vmem_limit_bytes=VMEM_LIMIT)


def _tile(n, pref):
    if n <= pref:
        return n
    t = pref
    while n % t:
        t //= 2
    return t


def _rmsnorm_kernel(x_ref, g_ref, o_ref):
    x = x_ref[...]
    ms = jnp.mean(x * x, axis=-1, keepdims=True)
    o_ref[...] = (x * lax.rsqrt(ms + EPS) * g_ref[...]).astype(o_ref.dtype)


def rmsnorm_rows(x, g):
    m, d = x.shape
    tm = _tile(m, 512)
    return pl.pallas_call(
        _rmsnorm_kernel,
        grid=(m // tm,),
        in_specs=[pl.BlockSpec((tm, d), lambda i: (i, 0)),
                  pl.BlockSpec((1, d), lambda i: (0, 0))],
        out_specs=pl.BlockSpec((tm, d), lambda i: (i, 0)),
        out_shape=jax.ShapeDtypeStruct((m, d), BF16),
        compiler_params=_cparams(("parallel",)),
        name="rmsnorm",
    )(x, g.reshape(1, d).astype(F32))


class _Out:
    def __init__(self, ref, transposed):
        self.ref = ref
        self.transposed = transposed

    def put(self, val, c0=None, c1=None):
        if self.transposed:
            val = val.T.astype(self.ref.dtype)
            if c0 is None:
                self.ref[...] = val
            else:
                self.ref[c0:c1, :] = val
        else:
            val = val.astype(self.ref.dtype)
            if c0 is None:
                self.ref[...] = val
            else:
                self.ref[:, c0:c1] = val


def _mm_kernel(*refs, pair_a, n_a, n_as, n_extra, n_extra_s, out_t, n_out_s, w_f32, epilogue):
    n_w = len(pair_a)
    n_out = len(out_t)
    a_refs = refs[:n_a]
    as_refs = refs[n_a:n_a + n_as]
    pos = n_a + n_as
    w_refs = refs[pos:pos + n_w]
    extras = refs[pos + n_w:pos + n_w + n_extra]
    extras_s = refs[pos + n_w + n_extra:pos + n_w + n_extra + n_extra_s]
    end = len(refs) - sum(w_f32)
    scr = list(refs[end:])
    outs_s = refs[end - n_out_s:end]
    out_refs = refs[end - n_out_s - n_out:end - n_out_s]
    w_use = [scr.pop(0) if f else w for w, f in zip(w_refs, w_f32)]

    @pl.when(pl.program_id(1) == 0)
    def _():
        for w, wb, f in zip(w_refs, w_use, w_f32):
            if f:
                wb[...] = w[...].astype(BF16)
        if n_as:
            s_vals = [a[...].astype(BF16) for a in as_refs]
            accs_s = [jnp.dot(s_vals[ai], wb[...], preferred_element_type=F32)
                      for ai, wb in zip(pair_a, w_use)]
            epilogue(accs_s, extras_s, [_Out(r, False) for r in outs_s])

    a_vals = [a[...].astype(BF16) for a in a_refs]
    accs = [jnp.dot(a_vals[ai], wb[...], preferred_element_type=F32)
            for ai, wb in zip(pair_a, w_use)]
    epilogue(accs, extras, [_Out(r, t) for r, t in zip(out_refs, out_t)])


def _rows_of(a):
    return a[0].shape[1] if isinstance(a, tuple) else a.shape[0]


def _a_spec(a, rows, first_block_only):
    if isinstance(a, tuple):
        arr, layer = a
        if first_block_only:
            return arr, pl.BlockSpec((None, rows, arr.shape[2]), lambda j, i, layer=layer: (layer, 0, 0))
        return arr, pl.BlockSpec((None, rows, arr.shape[2]), lambda j, i, layer=layer: (layer, i, 0))
    if first_block_only:
        return a, pl.BlockSpec((rows, a.shape[1]), lambda j, i: (0, 0))
    return a, pl.BlockSpec((rows, a.shape[1]), lambda j, i: (i, 0))


def fused_matmul(a_list, pairs, n, extras, outs, epilogue, *, tm, tn=1024, stacked=None, small=None,
                 name="mm"):
    m = _rows_of(a_list[0])
    tm = _tile(m, tm)
    tn = _tile(n, tn)
    grid = (n // tn, m // tm)
    a_list_s, extras_s, outs_s = small if small is not None else ([], [], [])
    ms = _rows_of(a_list_s[0]) if a_list_s else 0
    in_specs, args, scratch, w_f32 = [], [], [], []
    for a in a_list:
        arr, spec = _a_spec(a, tm, False)
        in_specs.append(spec)
        args.append(arr)
    for a in a_list_s:
        arr, spec = _a_spec(a, ms, True)
        in_specs.append(spec)
        args.append(arr)
    for _, w, layer, col0 in pairs:
        assert col0 % tn == 0
        k_dim = w.shape[-2]
        if w.ndim == 3:
            in_specs.append(pl.BlockSpec((None, k_dim, tn),
                                         lambda j, i, layer=layer, c=col0 // tn: (layer, 0, c + j)))
        else:
            in_specs.append(pl.BlockSpec((k_dim, tn), lambda j, i, c=col0 // tn: (0, c + j)))
        args.append(w)
        w_f32.append(w.dtype != BF16)
        if w_f32[-1]:
            scratch.append(pltpu.VMEM((k_dim, tn), BF16))
    for arr, kind, off in extras:
        if kind == "tile":
            in_specs.append(pl.BlockSpec((tm, tn), lambda j, i, off=off: (i, off + j)))
        else:
            in_specs.append(pl.BlockSpec(arr.shape, lambda j, i, nd=arr.ndim: (0,) * nd))
        args.append(arr)
    for arr, kind, off in extras_s:
        if kind == "tile":
            in_specs.append(pl.BlockSpec((ms, tn), lambda j, i, off=off: (0, off + j)))
        else:
            in_specs.append(pl.BlockSpec(arr.shape, lambda j, i, nd=arr.ndim: (0,) * nd))
        args.append(arr)
    out_specs, out_shape, aliases = [], [], {}
    for oi, (width, dt, kind) in enumerate(outs):
        if stacked is not None and stacked[0] == oi:
            _, layer, depth, buf = stacked
            assert kind == "n"
            out_specs.append(pl.BlockSpec((None, tm, tn), lambda j, i, layer=layer: (layer, i, j)))
            out_shape.append(jax.ShapeDtypeStruct((depth, m, n), dt))
            if buf is not None:
                assert buf.shape == (depth, m, n) and buf.dtype == dt
                in_specs.append(pl.BlockSpec(memory_space=pl.ANY))
                aliases[len(args)] = oi
                args.append(buf)
        elif kind == "n":
            out_specs.append(pl.BlockSpec((tm, tn), lambda j, i: (i, j)))
            out_shape.append(jax.ShapeDtypeStruct((m, n), dt))
        elif kind == "t":
            out_specs.append(pl.BlockSpec((tn, tm), lambda j, i: (j, i)))
            out_shape.append(jax.ShapeDtypeStruct((n, m), dt))
        else:
            assert grid[0] == 1
            out_specs.append(pl.BlockSpec((tm, width), lambda j, i: (i, 0)))
            out_shape.append(jax.ShapeDtypeStruct((m, width), dt))
    for width, dt, kind in outs_s:
        if kind == "n":
            out_specs.append(pl.BlockSpec((ms, tn), lambda j, i: (0, j)))
            out_shape.append(jax.ShapeDtypeStruct((ms, n), dt))
        else:
            assert kind == "whole" and grid[0] == 1
            out_specs.append(pl.BlockSpec((ms, width), lambda j, i: (0, 0)))
            out_shape.append(jax.ShapeDtypeStruct((ms, width), dt))
    kern = functools.partial(_mm_kernel, pair_a=tuple(p[0] for p in pairs), n_a=len(a_list),
                             n_as=len(a_list_s), n_extra=len(extras), n_extra_s=len(extras_s),
                             out_t=tuple(o[2] == "t" for o in outs), n_out_s=len(outs_s),
                             w_f32=tuple(w_f32), epilogue=epilogue)
    return pl.pallas_call(
        kern, grid=grid, in_specs=in_specs, out_specs=out_specs, out_shape=out_shape,
        scratch_shapes=scratch, input_output_aliases=aliases,
        compiler_params=_cparams(("parallel", "arbitrary")), name=name,
    )(*args)


def _ep_cast(accs, extras, outs):
    for o in outs:
        o.put(accs[0])


def _ep_groupnorm(accs, extras, outs, *, group, scale):
    acc = accs[0]
    g = extras[0][...] * scale
    for s in range(acc.shape[1] // group):
        z = acc[:, s * group:(s + 1) * group]
        ms = jnp.mean(z * z, axis=-1, keepdims=True)
        y = z * lax.rsqrt(ms + EPS) * g
        for o in outs:
            o.put(y, s * group, (s + 1) * group)


def _ep_gate(accs, extras, outs):
    y = jnp.dot(accs[0].astype(BF16), extras[0][...].astype(BF16),
                preferred_element_type=F32) + extras[1][...]
    ls = jnp.minimum(y, 0.0) - jnp.log1p(jnp.exp(-jnp.abs(y)))
    outs[0].put(ls * (1.0 / GATE_TAU))


def _ep_merge(accs, extras, outs):
    ga = extras[0][...].astype(F32)
    gb = extras[1][...].astype(F32)
    outs[0].put(jax.nn.sigmoid(ga) * accs[0] + jax.nn.sigmoid(gb) * accs[1])


def _ep_residual(accs, extras, outs):
    outs[0].put(extras[0][...] + accs[0])


def _ep_swiglu(accs, extras, outs):
    g = accs[0]
    outs[0].put(g * jax.nn.sigmoid(g) * accs[1])


def _ep_ple(accs, extras, outs):
    outs[0].put(extras[0][...] + jax.nn.sigmoid(accs[0]) * accs[1])


def _bucket_table(n, n_buckets):
    d = np.arange(n)
    nf = np.maximum(d, 1).astype(np.float64)
    large = MAX_EXACT + (np.log(nf / MAX_EXACT) / math.log(MAX_DISTANCE / MAX_EXACT)
                         * (n_buckets - MAX_EXACT)).astype(np.int64)
    large = np.minimum(large, n_buckets - 1)
    return np.where(d < MAX_EXACT, d, large).astype(np.int32)


def _toeplitz(w, t):
    n = 2 * t
    big = jnp.tile(w, (1, t))[:, :t * (n - 1)].reshape(w.shape[0], t, n - 1)
    return big[:, :, :t]


def _prompt_bias_tiles(rel_bias, t):
    n_buckets = rel_bias.shape[0]
    assert t + 1 >= MAX_DISTANCE
    table = _bucket_table(2 * t, n_buckets)
    rb = rel_bias.astype(F32)
    f = ((rb[table] - rb[n_buckets - 1][None, :]) * LOG2E).T
    neg = jnp.full((f.shape[0], t), NEG, F32)
    diag = _toeplitz(jnp.concatenate([f[:, :t], neg], axis=1), t)
    sub = _toeplitz(jnp.concatenate([f[:, t:], f[:, :t]], axis=1), t)
    return jnp.stack([diag, sub], axis=1)


def _decode_bias(rel_bias, page, n_heads):
    n_buckets = rel_bias.shape[0]
    assert page + 1 >= MAX_DISTANCE
    table = _bucket_table(page + 1, n_buckets)
    rel = (rel_bias.astype(F32) - rel_bias.astype(F32)[n_buckets - 1][None, :]) * LOG2E
    head_of_row = np.arange(2 * n_heads) % n_heads
    by_dist = rel[table][:, head_of_row].T
    last = jnp.repeat(by_dist[:, page - np.arange(page)], n_heads, axis=1)
    own = by_dist[:, 0:1]
    return last, own


def _attn_kernel(q_ref, k_ref, v_ref, bias_ref, lam_ref, g_ref, o_ref, m_sc, l_sc, acc_sc,
                 *, lam_init, dh, t, nh):
    qi = pl.program_id(2)
    m_sc[...] = jnp.full(m_sc.shape, NEG, F32)
    l_sc[...] = jnp.zeros(l_sc.shape, F32)
    acc_sc[...] = jnp.zeros(acc_sc.shape, F32)

    dv = 2 * dh
    maps = [(e, c) for e in range(nh) for c in range(2)]

    def tile(j, which):
        ks = pl.ds(pl.multiple_of(j * t, t), t)
        scores = [jnp.dot(k_ref[0, ks, e * dv + c * dh:e * dv + (c + 1) * dh],
                          q_ref[e * dv + c * dh:e * dv + (c + 1) * dh, :],
                          preferred_element_type=F32) for e, c in maps]
        probs, alphas = [], []
        for idx, (e, c) in enumerate(maps):
            s = scores[idx]
            if which is not None:
                s = s + bias_ref[e, which]
            m_prev = m_sc[idx]
            m_new = jnp.maximum(m_prev, jnp.max(s, axis=0, keepdims=True))
            alpha = jnp.exp2(m_prev - m_new)
            pr = jnp.exp2(s - m_new)
            l_sc[idx] = alpha * l_sc[idx] + jnp.sum(pr, axis=0, keepdims=True)
            m_sc[idx] = m_new
            probs.append(pr.astype(BF16))
            alphas.append(alpha)
        for idx, (e, c) in enumerate(maps):
            vt = v_ref[e * dv:(e + 1) * dv, ks]
            acc_sc[idx] = alphas[idx] * acc_sc[idx] + jnp.dot(vt, probs[idx], preferred_element_type=F32)

    def far(j, carry):
        tile(j, None)
        return carry

    lax.fori_loop(0, qi - 1, far, 0)

    @pl.when(qi >= 1)
    def _():
        tile(qi - 1, 1)

    tile(qi, 0)

    lv = lam_ref[...]
    lam = (jnp.exp(jnp.sum(lv[0:1] * lv[1:2], axis=-1, keepdims=True))
           - jnp.exp(jnp.sum(lv[2:3] * lv[3:4], axis=-1, keepdims=True)) + lam_init)
    for e in range(nh):
        r0 = 1.0 / l_sc[2 * e]
        r1 = lam / l_sc[2 * e + 1]
        ot = acc_sc[2 * e] * r0 - acc_sc[2 * e + 1] * r1
        ms = jnp.mean(ot * ot, axis=0, keepdims=True)
        yt = ot * lax.rsqrt(ms + EPS) * (g_ref[...] * (1.0 - lam_init))
        o_ref[0, :, e * dv:(e + 1) * dv] = yt.T.astype(o_ref.dtype)


def prompt_attention(qt, k, vt, bias, lam_vecs, diff_g, lam_init, n_heads, b, l):
    dtot = k.shape[2]
    dv = dtot // n_heads
    dh = dv // 2
    t = bias.shape[-1]
    nq = l // t
    nh = ATTN_HEADS if n_heads % ATTN_HEADS == 0 else 1
    dg = nh * dv
    kern = functools.partial(_attn_kernel, lam_init=lam_init, dh=dh, t=t, nh=nh)
    return pl.pallas_call(
        kern,
        grid=(b, n_heads // nh, nq),
        in_specs=[
            pl.BlockSpec((dg, t), lambda bi, h, qi: (h, bi * nq + qi)),
            pl.BlockSpec((1, l, dg), lambda bi, h, qi: (bi, 0, h)),
            pl.BlockSpec((dg, l), lambda bi, h, qi: (h, bi)),
            pl.BlockSpec((nh, 2, t, t), lambda bi, h, qi: (h, 0, 0, 0)),
            pl.BlockSpec((4, dh), lambda bi, h, qi: (0, 0)),
            pl.BlockSpec((dv, 1), lambda bi, h, qi: (0, 0)),
        ],
        out_specs=pl.BlockSpec((1, t, dg), lambda bi, h, qi: (bi, qi, h)),
        out_shape=jax.ShapeDtypeStruct((b, l, dtot), BF16),
        scratch_shapes=[pltpu.VMEM((2 * nh, 1, t), F32), pltpu.VMEM((2 * nh, 1, t), F32),
                        pltpu.VMEM((2 * nh, dv, t), F32)],
        compiler_params=_cparams(("parallel", "parallel", "arbitrary")),
        name="diff_attn_prompt",
    )(qt, k, vt, bias, lam_vecs, diff_g.reshape(dv, 1).astype(F32))


def _gla_kernel(q_ref, k_ref, v_ref, r_ref, g_ref, gn_ref, y_ref, s_ref, st_sc, *, chunk, nchunk, qscale):
    li = pl.program_id(1)
    nb = q_ref.shape[0]

    @pl.when(li == 0)
    def _():
        st_sc[...] = jnp.zeros(st_sc.shape, F32)

    lb = nchunk * chunk
    row = lax.broadcasted_iota(jnp.int32, (chunk, chunk), 0)
    col = lax.broadcasted_iota(jnp.int32, (chunk, chunk), 1)
    tril = (row >= col).astype(BF16)
    brow = lax.broadcasted_iota(jnp.int32, (lb, lb), 0)
    bcol = lax.broadcasted_iota(jnp.int32, (lb, lb), 1)
    block_causal = (brow >= bcol) & (brow // chunk == bcol // chunk)
    gn = gn_ref[...]
    nt = (((1,), (1,)), ((), ()))
    tn = (((0,), (0,)), ((), ()))

    pre = []
    for b in range(nb):
        g = g_ref[b]
        g_hi = g.astype(BF16)
        r1 = g - g_hi.astype(F32)
        g_mid = r1.astype(BF16)
        g_lo = (r1 - g_mid.astype(F32)).astype(BF16)
        cums, tots, mids = [], [], []
        mid = max(chunk // 2 - 1, 0)
        for c in range(nchunk):
            sl = slice(c * chunk, (c + 1) * chunk)
            bc = (jnp.dot(tril, g_hi[sl], preferred_element_type=F32)
                  + jnp.dot(tril, g_mid[sl], preferred_element_type=F32)
                  + jnp.dot(tril, g_lo[sl], preferred_element_type=F32))
            cums.append(bc)
            tots.append(jnp.broadcast_to(bc[chunk - 1:chunk, :], bc.shape))
            mids.append(jnp.broadcast_to(bc[mid:mid + 1, :], bc.shape))
        bcum = jnp.concatenate(cums, axis=0)
        btot = jnp.concatenate(tots, axis=0)
        bmid = jnp.concatenate(mids, axis=0)
        q = q_ref[b].astype(F32) * qscale
        k = k_ref[b].astype(F32)
        v = v_ref[b]
        qt = (q * jnp.exp(bcum)).astype(BF16)
        qa = (q * jnp.exp(bcum - bmid)).astype(BF16)
        kt = (k * jnp.exp(bmid - bcum)).astype(BF16)
        k2 = (k * jnp.exp(btot - bcum)).astype(BF16)
        a = lax.dot_general(qa, kt, nt, preferred_element_type=F32)
        a = jnp.where(block_causal, a, 0.0)
        o_intra = jnp.dot(a.astype(BF16), v, preferred_element_type=F32)
        pre.append((qt, k2, v, o_intra, jnp.exp(btot)))

    states = [st_sc[b] for b in range(nb)]
    outs = [[] for _ in range(nb)]
    for c in range(nchunk):
        sl = slice(c * chunk, (c + 1) * chunk)
        for b in range(nb):
            qt, k2, v, o_intra, decay = pre[b]
            st = states[b]
            outs[b].append(o_intra[sl] + lax.dot_general(qt[sl], st.astype(BF16), nt,
                                                         preferred_element_type=F32))
            states[b] = st * decay[c * chunk:c * chunk + 1, :] + lax.dot_general(
                v[sl], k2[sl], tn, preferred_element_type=F32)

    for b in range(nb):
        st_sc[b] = states[b]
        o = jnp.concatenate(outs[b], axis=0)
        ms = jnp.mean(o * o, axis=-1, keepdims=True)
        rr = r_ref[b].astype(F32)
        y = o * lax.rsqrt(ms + EPS) * gn * (rr * jax.nn.sigmoid(rr))
        y_ref[b] = y.astype(y_ref.dtype)

    @pl.when(li == pl.num_programs(1) - 1)
    def _():
        s_ref[:, 0] = st_sc[...]


def prompt_gla(z, log_a, gla_g, n_heads, dk, dv, off_q, off_k, off_v, off_r):
    b, l, _ = z.shape
    chunk = math.gcd(l, GLA_CHUNK)
    lb = _tile(l, GLA_BLOCK)
    nchunk = lb // chunk
    kern = functools.partial(_gla_kernel, chunk=chunk, nchunk=nchunk, qscale=dk ** -0.5)
    oq, ok, ov, orr = off_q // dk, off_k // dk, off_v // dv, off_r // dv
    return pl.pallas_call(
        kern,
        grid=(n_heads, l // lb),
        in_specs=[
            pl.BlockSpec((b, lb, dk), lambda h, i: (0, i, oq + h)),
            pl.BlockSpec((b, lb, dk), lambda h, i: (0, i, ok + h)),
            pl.BlockSpec((b, lb, dv), lambda h, i: (0, i, ov + h)),
            pl.BlockSpec((b, lb, dv), lambda h, i: (0, i, orr + h)),
            pl.BlockSpec((b, lb, dk), lambda h, i: (0, i, h)),
            pl.BlockSpec((1, dv), lambda h, i: (0, 0)),
        ],
        out_specs=[
            pl.BlockSpec((b, lb, dv), lambda h, i: (0, i, h)),
            pl.BlockSpec((b, 1, dv, dk), lambda h, i: (0, h, 0, 0)),
        ],
        out_shape=[jax.ShapeDtypeStruct((b, l, n_heads * dv), BF16),
                   jax.ShapeDtypeStruct((b, n_heads, dv, dk), F32)],
        scratch_shapes=[pltpu.VMEM((b, dv, dk), F32)],
        compiler_params=_cparams(("parallel", "arbitrary")),
        name="gla_prompt",
    )(z, z, z, z, log_a, gla_g.reshape(1, dv).astype(F32))


def _decode_attn_kernel(pt_ref, q_ref, *refs, n_sub, n_heads, lam_init):
    k_refs = refs[:n_sub]
    v_refs = refs[n_sub:2 * n_sub]
    (bias_last_ref, own_ref, kn_ref, vn_ref, lam_ref, g_ref, o_ref, m_sc, l_sc, acc_sc) = refs[2 * n_sub:]
    s_idx = pl.program_id(1)
    last = pl.num_programs(1) - 1

    @pl.when(s_idx == 0)
    def _():
        m_sc[...] = jnp.full(m_sc.shape, NEG, F32)
        l_sc[...] = jnp.zeros(l_sc.shape, F32)
        acc_sc[...] = jnp.zeros(acc_sc.shape, F32)

    q = q_ref[0]
    nrow = q.shape[0]

    def update(kmats, vmats, biases):
        parts = []
        for kmat, bias in zip(kmats, biases):
            sp = lax.dot_general(q, kmat, (((1,), (1,)), ((), ())), preferred_element_type=F32)
            parts.append(sp if bias is None else sp + bias)
        s = parts[0] if len(parts) == 1 else jnp.concatenate(parts, axis=1)
        r = s.shape[1]
        rowh = lax.broadcasted_iota(jnp.int32, (nrow, r), 0) % n_heads
        colh = lax.broadcasted_iota(jnp.int32, (nrow, r), 1) % n_heads
        s = jnp.where(rowh == colh, s, NEG)
        m_prev = m_sc[...]
        m_new = jnp.maximum(m_prev, jnp.max(s, axis=-1, keepdims=True))
        alpha = jnp.exp2(m_prev - m_new)
        pr = jnp.exp2(s - m_new).astype(BF16)
        l_sc[...] = alpha * l_sc[...] + jnp.sum(pr.astype(F32), axis=-1, keepdims=True)
        pv = None
        off = 0
        for vmat in vmats:
            d = jnp.dot(pr[:, off:off + vmat.shape[0]], vmat, preferred_element_type=F32)
            pv = d if pv is None else pv + d
            off += vmat.shape[0]
        acc_sc[...] = alpha * acc_sc[...] + pv
        m_sc[...] = m_new

    kmats, vmats = [], []
    for u in range(n_sub):
        kp = k_refs[u][...]
        vp = v_refs[u][...]
        kmats.append(kp.reshape(kp.shape[0] * kp.shape[1], kp.shape[2]).astype(BF16))
        vmats.append(vp.reshape(vp.shape[0] * vp.shape[1], vp.shape[2]).astype(BF16))
    is_last = (s_idx == last).astype(F32)
    update(kmats, vmats, [None] * (n_sub - 1) + [bias_last_ref[...] * is_last])

    @pl.when(s_idx == last)
    def _():
        update([kn_ref[0]], [vn_ref[0]], [own_ref[...]])
        lv = lam_ref[...]
        lam = (jnp.exp(jnp.sum(lv[0:1] * lv[1:2], axis=-1, keepdims=True))
               - jnp.exp(jnp.sum(lv[2:3] * lv[3:4], axis=-1, keepdims=True)) + lam_init)
        on = acc_sc[...] / l_sc[...]
        o = on[:n_heads] - lam * on[n_heads:]
        ms = jnp.mean(o * o, axis=-1, keepdims=True)
        o_ref[0] = o * lax.rsqrt(ms + EPS) * g_ref[...] * (1.0 - lam_init)


def decode_attention(layer, q, k_new, v_new, cache_k, cache_v, page_table, bias_last, bias_own, lam_vecs,
                     diff_g, lam_init):
    nb, n_pages = page_table.shape
    _, _, page, n_heads, dv = cache_v.shape
    dh = dv // 2
    n_sub = DECODE_PAGES if n_pages % DECODE_PAGES == 0 else 1
    n_steps = n_pages // n_sub
    rows = page * n_heads
    qh = q.reshape(nb, n_heads, 2 * dh)
    half = (np.arange(2 * dh) // dh)[None, :] == np.arange(2)[:, None]
    qall = (qh[:, None, :, :] * jnp.asarray(half, F32)[None, :, None, :]).reshape(nb, 2 * n_heads, 2 * dh)
    qall = qall.astype(BF16)
    kn = k_new.reshape(nb, n_heads, 2 * dh).astype(BF16)
    vn = v_new.reshape(nb, n_heads, dv).astype(BF16)
    pt_flat = page_table.reshape(-1).astype(jnp.int32)

    def page_spec(u):
        return pl.BlockSpec((None, None, page, n_heads, dv),
                            lambda b, s, pt, u=u: (layer, pt[b * n_pages + s * n_sub + u], 0, 0, 0))

    const2 = lambda b, s, pt: (0, 0)
    grid_spec = pltpu.PrefetchScalarGridSpec(
        num_scalar_prefetch=1,
        grid=(nb, n_steps),
        in_specs=[pl.BlockSpec((1, 2 * n_heads, 2 * dh), lambda b, s, pt: (b, 0, 0))]
        + [page_spec(u) for u in range(n_sub)] + [page_spec(u) for u in range(n_sub)]
        + [pl.BlockSpec((2 * n_heads, rows), const2),
           pl.BlockSpec((2 * n_heads, 1), const2),
           pl.BlockSpec((1, n_heads, 2 * dh), lambda b, s, pt: (b, 0, 0)),
           pl.BlockSpec((1, n_heads, dv), lambda b, s, pt: (b, 0, 0)),
           pl.BlockSpec((4, dh), const2),
           pl.BlockSpec((1, dv), const2)],
        out_specs=pl.BlockSpec((1, n_heads, dv), lambda b, s, pt: (b, 0, 0)),
        scratch_shapes=[pltpu.VMEM((2 * n_heads, 1), F32), pltpu.VMEM((2 * n_heads, 1), F32),
                        pltpu.VMEM((2 * n_heads, dv), F32)],
    )
    kern = functools.partial(_decode_attn_kernel, n_sub=n_sub, n_heads=n_heads, lam_init=lam_init)
    return pl.pallas_call(
        kern, grid_spec=grid_spec,
        out_shape=jax.ShapeDtypeStruct((nb, n_heads, dv), F32),
        compiler_params=_cparams(("parallel", "arbitrary")),
        name="diff_attn_decode",
    )(pt_flat, qall, *([cache_k] * n_sub), *([cache_v] * n_sub), bias_last, bias_own, kn, vn,
      lam_vecs, diff_g.reshape(1, dv).astype(F32))


def _gla_step_kernel(s_ref, a_ref, k_ref, q_ref, v_ref, r_ref, gn_ref, *rest):
    s_out_ref, y_ref = rest[-2:]
    s_new = jnp.exp(a_ref[0, 0]) * s_ref[...] + k_ref[0, 0] * v_ref[0, 0]
    s_out_ref[...] = s_new
    o = jnp.sum(q_ref[0, 0] * s_new, axis=0, keepdims=True)
    ms = jnp.mean(o * o, axis=-1, keepdims=True)
    rr = r_ref[0, 0]
    y_ref[0, 0] = o * lax.rsqrt(ms + EPS) * gn_ref[...] * (rr * jax.nn.sigmoid(rr))


def decode_gla(layer, state, new_states, log_a, k, q, v, r, gla_g):
    _, nb, n_heads, dk, dv = state.shape
    col = lambda t: t.reshape(nb, n_heads, dk, 1)
    rowv = lambda t: t.reshape(nb, n_heads, 1, dv)
    cspec = pl.BlockSpec((1, 1, dk, 1), lambda b, h: (b, h, 0, 0))
    rspec = pl.BlockSpec((1, 1, 1, dv), lambda b, h: (b, h, 0, 0))
    sspec = pl.BlockSpec((None, None, None, dk, dv), lambda b, h: (layer, b, h, 0, 0))
    in_specs = [sspec, cspec, cspec, cspec, rspec, rspec, pl.BlockSpec((1, dv), lambda b, h: (0, 0))]
    args = [state, col(log_a), col(k), col(q), rowv(v), rowv(r), gla_g.reshape(1, dv).astype(F32)]
    aliases = {}
    if new_states is not None:
        in_specs.append(pl.BlockSpec(memory_space=pl.ANY))
        aliases[len(args)] = 0
        args.append(new_states)
    return pl.pallas_call(
        _gla_step_kernel,
        grid=(nb, n_heads),
        in_specs=in_specs,
        out_specs=[sspec, rspec],
        out_shape=[jax.ShapeDtypeStruct(state.shape, F32),
                   jax.ShapeDtypeStruct((nb, n_heads, 1, dv), F32)],
        input_output_aliases=aliases,
        compiler_params=_cparams(("parallel", "parallel")),
        name="gla_decode",
    )(*args)


def _split_w_in(w_in, sizes):
    offs = [int(o) for o in np.concatenate([[0], np.cumsum(sizes)])]
    depth = w_in.shape[0]
    main = [w_in[i, :, :offs[7]].astype(BF16) for i in range(depth)]
    gates = [w_in[i, :, offs[8]:offs[10]].astype(BF16) for i in range(depth)]
    glow = [w_in[i, :, offs[7]:offs[8]].astype(BF16) for i in range(depth)]
    return offs, main, gates, glow


def _layer_weights(i, split, sizes, others):
    offs, main, gates, glow = split
    w = {"q": (main[i], None, offs[0], sizes[0]), "k": (main[i], None, offs[1], sizes[1]),
         "v": (main[i], None, offs[2], sizes[2]), "rest": (main[i], None, offs[3], offs[7] - offs[3]),
         "gates": (gates[i], None, 0, offs[10] - offs[8]), "glow": (glow[i], None, 0, sizes[7])}
    for name, arr in others.items():
        w[name] = (arr, i, 0, arr.shape[2])
    return w


def _pair(a_idx, wspec):
    return (a_idx, wspec[0], wspec[1], wspec[2])


def _dense_front(xp, xs, lw, g_mix, q_g, k_g, gate_w2, gate_b, dh, stacks):
    hp = rmsnorm_rows(xp, g_mix)
    hs = rmsnorm_rows(xs, g_mix)
    layer, depth, kbuf, vbuf = stacks
    d_qk = lw["q"][3]
    d_v = lw["v"][3]
    qn = functools.partial(_ep_groupnorm, group=dh, scale=dh ** -0.5 * LOG2E)
    kn = functools.partial(_ep_groupnorm, group=dh, scale=1.0)
    gq = (q_g.reshape(1, dh).astype(F32), "full", 0)
    gk = (k_g.reshape(1, dh).astype(F32), "full", 0)
    tm = ROW_TILE
    qt, q_s = fused_matmul([hp], [_pair(0, lw["q"])], d_qk, [gq], [(d_qk, BF16, "t")], qn, tm=tm,
                           small=([hs], [gq], [(d_qk, BF16, "n")]), name="in_q")
    kbuf, k16, k_s = fused_matmul([hp], [_pair(0, lw["k"])], d_qk, [gk], [(d_qk, F32, "n"), (d_qk, BF16, "n")],
                                  kn, tm=tm, stacked=(0, layer, depth, kbuf),
                                  small=([hs], [gk], [(d_qk, F32, "n")]), name="in_k")
    vbuf, vt, v_s = fused_matmul([hp], [_pair(0, lw["v"])], d_v, [], [(d_v, F32, "n"), (d_v, BF16, "t")],
                                 _ep_cast, tm=tm, stacked=(0, layer, depth, vbuf),
                                 small=([hs], [], [(d_v, F32, "n")]), name="in_v")
    n_rest = lw["rest"][3]
    rest, rest_s = fused_matmul([hp], [_pair(0, lw["rest"])], n_rest, [], [(n_rest, BF16, "n")], _ep_cast,
                                tm=tm, small=([hs], [], [(n_rest, BF16, "n")]), name="in_rest")
    n_gates = lw["gates"][3]
    gates, gates_s = fused_matmul([hp], [_pair(0, lw["gates"])], n_gates, [], [(n_gates, BF16, "n")], _ep_cast,
                                  tm=tm, small=([hs], [], [(n_gates, BF16, "n")]), name="in_gates")
    n_gate = gate_w2.shape[1]
    gate_extras = [(gate_w2.astype(F32), "full", 0), (gate_b.reshape(1, n_gate).astype(F32), "full", 0)]
    log_a, log_a_s = fused_matmul([hp], [_pair(0, lw["glow"])], lw["glow"][3], gate_extras,
                                  [(n_gate, F32, "whole")], _ep_gate, tm=tm,
                                  small=([hs], gate_extras, [(n_gate, F32, "whole")]), name="in_gate")
    return ((qt, kbuf, k16, vbuf, vt, rest, gates, log_a), (q_s, k_s, v_s, rest_s, gates_s, log_a_s))


def _dense_back(xp, xs, y_p, y_s, gates_p, gates_s, p_p, p_s, lw, g_ffn, g_ple):
    d = xp.shape[1]
    tm = ROW_TILE
    tn = _tile(d, COL_TILE_NARROW)
    gate_tiles = lambda g: [(g, "tile", 0), (g, "tile", d // tn)]
    merged, merged_s = fused_matmul(list(y_p), [_pair(0, lw["w_pa"]), _pair(1, lw["w_pb"])], d,
                                    gate_tiles(gates_p), [(d, BF16, "n")], _ep_merge, tm=tm, tn=tn,
                                    small=(list(y_s), gate_tiles(gates_s), [(d, BF16, "n")]), name="merge")
    xp, xs = fused_matmul([merged], [_pair(0, lw["w_o"])], d, [(xp, "tile", 0)], [(d, F32, "n")],
                          _ep_residual, tm=tm, tn=2 * tn,
                          small=([merged_s], [(xs, "tile", 0)], [(d, F32, "n")]), name="out_proj")
    hp = rmsnorm_rows(xp, g_ffn)
    hs = rmsnorm_rows(xs, g_ffn)
    d_ff = lw["w_gate"][3]
    gu, gu_s = fused_matmul([hp], [_pair(0, lw["w_gate"]), _pair(0, lw["w_up"])], d_ff, [], [(d_ff, BF16, "n")],
                            _ep_swiglu, tm=tm, tn=COL_TILE_NARROW,
                            small=([hs], [], [(d_ff, BF16, "n")]), name="ffn_up")
    xp, xs = fused_matmul([gu], [_pair(0, lw["w_down"])], d, [(xp, "tile", 0)], [(d, F32, "n")], _ep_residual,
                          tm=tm // 2, tn=COL_TILE_NARROW,
                          small=([gu_s], [(xs, "tile", 0)], [(d, F32, "n")]), name="ffn_down")
    hp = rmsnorm_rows(xp, g_ple)
    hs = rmsnorm_rows(xs, g_ple)
    xp, xs = fused_matmul([hp, p_p], [_pair(0, lw["w_ple_gate"]), _pair(1, lw["w_ple_proj"])], d,
                          [(xp, "tile", 0)], [(d, F32, "n")], _ep_ple, tm=tm, tn=tn,
                          small=([hs, p_s], [(xs, "tile", 0)], [(d, F32, "n")]), name="ple")
    return xp, xs


def kernel(x_prompt, x_sample, cache_k, cache_v, state_gla, page_table, p_prompt, p_sample, rel_bias, norm_mix_g, w_in, gla_gate_w2, gla_gate_b, q_norm_g, k_norm_g, lambda_q1, lambda_k1, lambda_q2, lambda_k2, diff_norm_g, gla_norm_g, w_pa, w_pb, w_o, norm_ffn_g, w_gate, w_up, w_down, norm_ple_g, w_ple_gate, w_ple_proj):
    depth = w_in.shape[0]
    bp, lp, d = x_prompt.shape
    bs, ls, _ = x_sample.shape
    assert ls == 1
    _, _, page, h_a, dv_a = cache_v.shape
    dh_a = dv_a // 2
    d_a = h_a * dv_a
    _, _, h_b, dk_b, dv_b = state_gla.shape
    d_b = h_b * dv_b
    rank = gla_gate_w2.shape[1]
    sizes = [d_a, d_a, d_a, h_b * dk_b, h_b * dk_b, d_b, d_b, rank, d, d]
    off_qb, off_kb = 0, h_b * dk_b
    off_vb = 2 * h_b * dk_b
    off_r = off_vb + d_b
    others = {"w_pa": w_pa, "w_pb": w_pb, "w_o": w_o, "w_gate": w_gate, "w_up": w_up, "w_down": w_down,
              "w_ple_gate": w_ple_gate, "w_ple_proj": w_ple_proj}

    mp = bp * lp
    ms = SAMPLE_ROWS
    xp = x_prompt.reshape(mp, d)
    xs = jnp.zeros((ms, d), F32).at[:bs].set(x_sample.reshape(bs, d))
    attn_bias = _prompt_bias_tiles(rel_bias, _tile(lp, ATTN_TILE))
    dec_bias_last, dec_bias_own = _decode_bias(rel_bias, page, h_a)
    split = _split_w_in(w_in, sizes)
    p_prompt3 = p_prompt.reshape(depth, mp, -1)
    p_sample3 = jnp.zeros((depth, ms, p_sample.shape[-1]), F32).at[:, :bs].set(p_sample.reshape(depth, bs, -1))
    kbuf = vbuf = sbuf = None

    sp_l, ks_l, vs_l = [], [], []
    for i in range(depth):
        lam_init = 0.8 - 0.6 * math.exp(-0.3 * i)
        lw = _layer_weights(i, split, sizes, others)
        lam_vecs = jnp.stack([lambda_q1[i], lambda_k1[i], lambda_q2[i], lambda_k2[i]]).astype(F32)

        (qt, kbuf, k16, vbuf, vt, rest, gates, log_a), (q_s, k_s, v_s, rest_s, gates_s, log_a_s) = _dense_front(
            xp, xs, lw, norm_mix_g[i], q_norm_g[i], k_norm_g[i], gla_gate_w2[i], gla_gate_b[i], dh_a,
            (i, depth, kbuf, vbuf))

        to3 = lambda t: t.reshape(bp, lp, t.shape[1])
        ya = prompt_attention(qt, to3(k16), vt, attn_bias, lam_vecs, diff_norm_g[i], lam_init, h_a, bp, lp)
        yb, st = prompt_gla(to3(rest), to3(log_a), gla_norm_g[i], h_b, dk_b, dv_b, off_qb, off_kb, off_vb, off_r)
        sp_l.append(jnp.swapaxes(st, 2, 3))

        ya_s = decode_attention(i, q_s[:bs].astype(F32), k_s[:bs], v_s[:bs], cache_k, cache_v, page_table,
                                dec_bias_last, dec_bias_own, lam_vecs, diff_norm_g[i], lam_init)
        rest32 = rest_s[:bs].astype(F32)
        hk = lambda t: t.reshape(bs, h_b, -1)
        sbuf, yb_s = decode_gla(i, state_gla, sbuf, hk(log_a_s[:bs]),
                                hk(rest32[:, off_kb:off_kb + h_b * dk_b]),
                                hk(rest32[:, off_qb:off_qb + h_b * dk_b]) * (dk_b ** -0.5),
                                hk(rest32[:, off_vb:off_vb + d_b]), hk(rest32[:, off_r:off_r + d_b]),
                                gla_norm_g[i])
        pad = lambda t: jnp.zeros((ms, t.shape[1]), BF16).at[:bs].set(t.astype(BF16))

        xp, xs = _dense_back(xp, xs, (ya.reshape(mp, d_a), yb.reshape(mp, d_b)),
                             (pad(ya_s.reshape(bs, d_a)), pad(yb_s.reshape(bs, d_b))), gates, gates_s,
                             (p_prompt3, i), (p_sample3, i), lw, norm_ffn_g[i], norm_ple_g[i])
        ks_l.append(k_s[:bs].reshape(bs, 1, h_a, dv_a))
        vs_l.append(v_s[:bs].reshape(bs, 1, h_a, dv_a))

    return (xp.reshape(bp, lp, d), xs[:bs].reshape(bs, 1, d),
            kbuf.reshape(depth, bp, lp, h_a, dv_a), vbuf.reshape(depth, bp, lp, h_a, dv_a),
            jnp.stack(sp_l), jnp.stack(ks_l), jnp.stack(vs_l), sbuf)
```

```python
import functools
import math

import numpy as np
import jax
import jax.numpy as jnp
from jax import lax
from jax.experimental import pallas as pl
from jax.experimental.pallas import tpu as pltpu

EPS = 1e-6
GATE_TAU = 16.0
GLA_CHUNK = 64
MAX_EXACT = 16
MAX_DISTANCE = 128
LOG2E = math.log2(math.e)
NEG = -1e30
VMEM_LIMIT = 56 * 1024 * 1024
ATTN_TILE = 512
ATTN_HEADS = 2
GLA_BLOCK = 512
ROW_TILE = 1024
COL_TILE_NARROW = 512
SAMPLE_ROWS = 16
DECODE_PAGES = 8

BF16 = jnp.bfloat16
F32 = jnp.float32


def _cparams(sem):
    return pltpu.CompilerParams(dimension_semantics=sem, vmem_limit_bytes=VMEM_LIMIT)


def _tile(n, pref):
    if n <= pref:
        return n
    t = pref
    while n % t:
        t //= 2
    return t


def _rmsnorm_kernel(x_ref, g_ref, o_ref):
    x = x_ref[...]
    ms = jnp.mean(x * x, axis=-1, keepdims=True)
    o_ref[...] = (x * lax.rsqrt(ms + EPS) * g_ref[...]).astype(o_ref.dtype)


def rmsnorm_rows(x, g):
    m, d = x.shape
    tm = _tile(m, 512)
    return pl.pallas_call(
        _rmsnorm_kernel,
        grid=(m // tm,),
        in_specs=[pl.BlockSpec((tm, d), lambda i: (i, 0)),
                  pl.BlockSpec((1, d), lambda i: (0, 0))],
        out_specs=pl.BlockSpec((tm, d), lambda i: (i, 0)),
        out_shape=jax.ShapeDtypeStruct((m, d), BF16),
        compiler_params=_cparams(("parallel",)),
        name="rmsnorm",
    )(x, g.reshape(1, d).astype(F32))


class _Out:
    def __init__(self, ref, transposed):
        self.ref = ref
        self.transposed = transposed

    def put(self, val, c0=None, c1=None):
        if self.transposed:
            val = val.T.astype(self.ref.dtype)
            if c0 is None:
                self.ref[...] = val
            else:
                self.ref[c0:c1, :] = val
        else:
            val = val.astype(self.ref.dtype)
            if c0 is None:
                self.ref[...] = val
            else:
                self.ref[:, c0:c1] = val


def _mm_kernel(*refs, pair_a, n_a, n_as, n_extra, n_extra_s, out_t, n_out_s, w_f32, epilogue):
    n_w = len(pair_a)
    n_out = len(out_t)
    a_refs = refs[:n_a]
    as_refs = refs[n_a:n_a + n_as]
    pos = n_a + n_as
    w_refs = refs[pos:pos + n_w]
    extras = refs[pos + n_w:pos + n_w + n_extra]
    extras_s = refs[pos + n_w + n_extra:pos + n_w + n_extra + n_extra_s]
    end = len(refs) - sum(w_f32)
    scr = list(refs[end:])
    outs_s = refs[end - n_out_s:end]
    out_refs = refs[end - n_out_s - n_out:end - n_out_s]
    w_use = [scr.pop(0) if f else w for w, f in zip(w_refs, w_f32)]

    @pl.when(pl.program_id(1) == 0)
    def _():
        for w, wb, f in zip(w_refs, w_use, w_f32):
            if f:
                wb[...] = w[...].astype(BF16)
        if n_as:
            s_vals = [a[...].astype(BF16) for a in as_refs]
            accs_s = [jnp.dot(s_vals[ai], wb[...], preferred_element_type=F32)
                      for ai, wb in zip(pair_a, w_use)]
            epilogue(accs_s, extras_s, [_Out(r, False) for r in outs_s])

    a_vals = [a[...].astype(BF16) for a in a_refs]
    accs = [jnp.dot(a_vals[ai], wb[...], preferred_element_type=F32)
            for ai, wb in zip(pair_a, w_use)]
    epilogue(accs, extras, [_Out(r, t) for r, t in zip(out_refs, out_t)])


def _rows_of(a):
    return a[0].shape[1] if isinstance(a, tuple) else a.shape[0]


def _a_spec(a, rows, first_block_only):
    if isinstance(a, tuple):
        arr, layer = a
        if first_block_only:
            return arr, pl.BlockSpec((None, rows, arr.shape[2]), lambda j, i, layer=layer: (layer, 0, 0))
        return arr, pl.BlockSpec((None, rows, arr.shape[2]), lambda j, i, layer=layer: (layer, i, 0))
    if first_block_only:
        return a, pl.BlockSpec((rows, a.shape[1]), lambda j, i: (0, 0))
    return a, pl.BlockSpec((rows, a.shape[1]), lambda j, i: (i, 0))


def fused_matmul(a_list, pairs, n, extras, outs, epilogue, *, tm, tn=1024, stacked=None, small=None,
                 name="mm"):
    m = _rows_of(a_list[0])
    tm = _tile(m, tm)
    tn = _tile(n, tn)
    grid = (n // tn, m // tm)
    a_list_s, extras_s, outs_s = small if small is not None else ([], [], [])
    ms = _rows_of(a_list_s[0]) if a_list_s else 0
    in_specs, args, scratch, w_f32 = [], [], [], []
    for a in a_list:
        arr, spec = _a_spec(a, tm, False)
        in_specs.append(spec)
        args.append(arr)
    for a in a_list_s:
        arr, spec = _a_spec(a, ms, True)
        in_specs.append(spec)
        args.append(arr)
    for _, w, layer, col0 in pairs:
        assert col0 % tn == 0
        k_dim = w.shape[-2]
        if w.ndim == 3:
            in_specs.append(pl.BlockSpec((None, k_dim, tn),
                                         lambda j, i, layer=layer, c=col0 // tn: (layer, 0, c + j)))
        else:
            in_specs.append(pl.BlockSpec((k_dim, tn), lambda j, i, c=col0 // tn: (0, c + j)))
        args.append(w)
        w_f32.append(w.dtype != BF16)
        if w_f32[-1]:
            scratch.append(pltpu.VMEM((k_dim, tn), BF16))
    for arr, kind, off in extras:
        if kind == "tile":
            in_specs.append(pl.BlockSpec((tm, tn), lambda j, i, off=off: (i, off + j)))
        else:
            in_specs.append(pl.BlockSpec(arr.shape, lambda j, i, nd=arr.ndim: (0,) * nd))
        args.append(arr)
    for arr, kind, off in extras_s:
        if kind == "tile":
            in_specs.append(pl.BlockSpec((ms, tn), lambda j, i, off=off: (0, off + j)))
        else:
            in_specs.append(pl.BlockSpec(arr.shape, lambda j, i, nd=arr.ndim: (0,) * nd))
        args.append(arr)
    out_specs, out_shape, aliases = [], [], {}
    for oi, (width, dt, kind) in enumerate(outs):
        if stacked is not None and stacked[0] == oi:
            _, layer, depth, buf = stacked
            assert kind == "n"
            out_specs.append(pl.BlockSpec((None, tm, tn), lambda j, i, layer=layer: (layer, i, j)))
            out_shape.append(jax.ShapeDtypeStruct((depth, m, n), dt))
            if buf is not None:
                assert buf.shape == (depth, m, n) and buf.dtype == dt
                in_specs.append(pl.BlockSpec(memory_space=pl.ANY))
                aliases[len(args)] = oi
                args.append(buf)
        elif kind == "n":
            out_specs.append(pl.BlockSpec((tm, tn), lambda j, i: (i, j)))
            out_shape.append(jax.ShapeDtypeStruct((m, n), dt))
        elif kind == "t":
            out_specs.append(pl.BlockSpec((tn, tm), lambda j, i: (j, i)))
            out_shape.append(jax.ShapeDtypeStruct((n, m), dt))
        else:
            assert grid[0] == 1
            out_specs.append(pl.BlockSpec((tm, width), lambda j, i: (i, 0)))
            out_shape.append(jax.ShapeDtypeStruct((m, width), dt))
    for width, dt, kind in outs_s:
        if kind == "n":
            out_specs.append(pl.BlockSpec((ms, tn), lambda j, i: (0, j)))
            out_shape.append(jax.ShapeDtypeStruct((ms, n), dt))
        else:
            assert kind == "whole" and grid[0] == 1
            out_specs.append(pl.BlockSpec((ms, width), lambda j, i: (0, 0)))
            out_shape.append(jax.ShapeDtypeStruct((ms, width), dt))
    kern = functools.partial(_mm_kernel, pair_a=tuple(p[0] for p in pairs), n_a=len(a_list),
                             n_as=len(a_list_s), n_extra=len(extras), n_extra_s=len(extras_s),
                             out_t=tuple(o[2] == "t" for o in outs), n_out_s=len(outs_s),
                             w_f32=tuple(w_f32), epilogue=epilogue)
    return pl.pallas_call(
        kern, grid=grid, in_specs=in_specs, out_specs=out_specs, out_shape=out_shape,
        scratch_shapes=scratch, input_output_aliases=aliases,
        compiler_params=_cparams(("parallel", "arbitrary")), name=name,
    )(*args)


def _ep_cast(accs, extras, outs):
    for o in outs:
        o.put(accs[0])


def _ep_groupnorm(accs, extras, outs, *, group, scale):
    acc = accs[0]
    g = extras[0][...] * scale
    for s in range(acc.shape[1] // group):
        z = acc[:, s * group:(s + 1) * group]
        ms = jnp.mean(z * z, axis=-1, keepdims=True)
        y = z * lax.rsqrt(ms + EPS) * g
        for o in outs:
            o.put(y, s * group, (s + 1) * group)


def _ep_gate(accs, extras, outs):
    y = jnp.dot(accs[0].astype(BF16), extras[0][...].astype(BF16),
                preferred_element_type=F32) + extras[1][...]
    ls = jnp.minimum(y, 0.0) - jnp.log1p(jnp.exp(-jnp.abs(y)))
    outs[0].put(ls * (1.0 / GATE_TAU))


def _ep_merge(accs, extras, outs):
    ga = extras[0][...].astype(F32)
    gb = extras[1][...].astype(F32)
    outs[0].put(jax.nn.sigmoid(ga) * accs[0] + jax.nn.sigmoid(gb) * accs[1])


def _ep_residual(accs, extras, outs):
    outs[0].put(extras[0][...] + accs[0])


def _ep_swiglu(accs, extras, outs):
    g = accs[0]
    outs[0].put(g * jax.nn.sigmoid(g) * accs[1])


def _ep_ple(accs, extras, outs):
    outs[0].put(extras[0][...] + jax.nn.sigmoid(accs[0]) * accs[1])


def _bucket_table(n, n_buckets):
    d = np.arange(n)
    nf = np.maximum(d, 1).astype(np.float64)
    large = MAX_EXACT + (np.log(nf / MAX_EXACT) / math.log(MAX_DISTANCE / MAX_EXACT)
                         * (n_buckets - MAX_EXACT)).astype(np.int64)
    large = np.minimum(large, n_buckets - 1)
    return np.where(d < MAX_EXACT, d, large).astype(np.int32)


def _toeplitz(w, t):
    n = 2 * t
    big = jnp.tile(w, (1, t))[:, :t * (n - 1)].reshape(w.shape[0], t, n - 1)
    return big[:, :, :t]


def _prompt_bias_tiles(rel_bias, t):
    n_buckets = rel_bias.shape[0]
    assert t + 1 >= MAX_DISTANCE
    table = _bucket_table(2 * t, n_buckets)
    rb = rel_bias.astype(F32)
    f = ((rb[table] - rb[n_buckets - 1][None, :]) * LOG2E).T
    neg = jnp.full((f.shape[0], t), NEG, F32)
    diag = _toeplitz(jnp.concatenate([f[:, :t], neg], axis=1), t)
    sub = _toeplitz(jnp.concatenate([f[:, t:], f[:, :t]], axis=1), t)
    return jnp.stack([diag, sub], axis=1)


def _decode_bias(rel_bias, page, n_heads):
    n_buckets = rel_bias.shape[0]
    assert page + 1 >= MAX_DISTANCE
    table = _bucket_table(page + 1, n_buckets)
    rel = (rel_bias.astype(F32) - rel_bias.astype(F32)[n_buckets - 1][None, :]) * LOG2E
    head_of_row = np.arange(2 * n_heads) % n_heads
    by_dist = rel[table][:, head_of_row].T
    last = jnp.repeat(by_dist[:, page - np.arange(page)], n_heads, axis=1)
    own = by_dist[:, 0:1]
    return last, own


def _attn_kernel(q_ref, k_ref, v_ref, bias_ref, lam_ref, g_ref, o_ref, m_sc, l_sc, acc_sc,
                 *, lam_init, dh, t, nh):
    qi = pl.program_id(2)
    m_sc[...] = jnp.full(m_sc.shape, NEG, F32)
    l_sc[...] = jnp.zeros(l_sc.shape, F32)
    acc_sc[...] = jnp.zeros(acc_sc.shape, F32)

    dv = 2 * dh
    maps = [(e, c) for e in range(nh) for c in range(2)]

    def tile(j, which, k0=0, kn=t, q0=0):
        ks = pl.ds(pl.multiple_of(j * t + k0, math.gcd(t, k0) if k0 else t), kn)
        scores = [jnp.dot(k_ref[0, ks, e * dv + c * dh:e * dv + (c + 1) * dh],
                          q_ref[e * dv + c * dh:e * dv + (c + 1) * dh, q0:],
                          preferred_element_type=F32) for e, c in maps]
        probs, alphas = [], []
        for idx, (e, c) in enumerate(maps):
            s = scores[idx]
            if which is not None:
                s = s + bias_ref[e, which, k0:k0 + kn, q0:]
            m_prev = m_sc[idx, :, q0:]
            m_new = jnp.maximum(m_prev, jnp.max(s, axis=0, keepdims=True))
            alpha = jnp.exp2(m_prev - m_new)
            pr = jnp.exp2(s - m_new)
            l_sc[idx, :, q0:] = alpha * l_sc[idx, :, q0:] + jnp.sum(pr, axis=0, keepdims=True)
            m_sc[idx, :, q0:] = m_new
            probs.append(pr.astype(BF16))
            alphas.append(alpha)
        for idx, (e, c) in enumerate(maps):
            vt = v_ref[e * dv:(e + 1) * dv, ks]
            acc_sc[idx, :, q0:] = alphas[idx] * acc_sc[idx, :, q0:] + jnp.dot(
                vt, probs[idx], preferred_element_type=F32)

    def far(j, carry):
        tile(j, None)
        return carry

    lax.fori_loop(0, qi - 1, far, 0)

    @pl.when(qi >= 1)
    def _():
        tile(qi - 1, 1)

    half = t // 2
    if half % 128 == 0:
        tile(qi, 0, 0, half, 0)
        tile(qi, 0, half, half, half)
    else:
        tile(qi, 0)

    lv = lam_ref[...]
    lam = (jnp.exp(jnp.sum(lv[0:1] * lv[1:2], axis=-1, keepdims=True))
           - jnp.exp(jnp.sum(lv[2:3] * lv[3:4], axis=-1, keepdims=True)) + lam_init)
    for e in range(nh):
        r0 = 1.0 / l_sc[2 * e]
        r1 = lam / l_sc[2 * e + 1]
        ot = acc_sc[2 * e] * r0 - acc_sc[2 * e + 1] * r1
        ms = jnp.mean(ot * ot, axis=0, keepdims=True)
        yt = ot * lax.rsqrt(ms + EPS) * (g_ref[...] * (1.0 - lam_init))
        o_ref[0, :, e * dv:(e + 1) * dv] = yt.T.astype(o_ref.dtype)


def prompt_attention(qt, k, vt, bias, lam_vecs, diff_g, lam_init, n_heads, b, l):
    dtot = k.shape[2]
    dv = dtot // n_heads
    dh = dv // 2
    t = bias.shape[-1]
    nq = l // t
    nh = ATTN_HEADS if n_heads % ATTN_HEADS == 0 else 1
    dg = nh * dv
    kern = functools.partial(_attn_kernel, lam_init=lam_init, dh=dh, t=t, nh=nh)
    return pl.pallas_call(
        kern,
        grid=(b, n_heads // nh, nq),
        in_specs=[
            pl.BlockSpec((dg, t), lambda bi, h, qi: (h, bi * nq + qi)),
            pl.BlockSpec((1, l, dg), lambda bi, h, qi: (bi, 0, h)),
            pl.BlockSpec((dg, l), lambda bi, h, qi: (h, bi)),
            pl.BlockSpec((nh, 2, t, t), lambda bi, h, qi: (h, 0, 0, 0)),
            pl.BlockSpec((4, dh), lambda bi, h, qi: (0, 0)),
            pl.BlockSpec((dv, 1), lambda bi, h, qi: (0, 0)),
        ],
        out_specs=pl.BlockSpec((1, t, dg), lambda bi, h, qi: (bi, qi, h)),
        out_shape=jax.ShapeDtypeStruct((b, l, dtot), BF16),
        scratch_shapes=[pltpu.VMEM((2 * nh, 1, t), F32), pltpu.VMEM((2 * nh, 1, t), F32),
                        pltpu.VMEM((2 * nh, dv, t), F32)],
        compiler_params=_cparams(("parallel", "parallel", "arbitrary")),
        name="diff_attn_prompt",
    )(qt, k, vt, bias, lam_vecs, diff_g.reshape(dv, 1).astype(F32))


def _gla_kernel(q_ref, k_ref, v_ref, r_ref, g_ref, gn_ref, y_ref, s_ref, st_sc, *, chunk, nchunk, qscale):
    li = pl.program_id(1)
    nb = q_ref.shape[0]

    @pl.when(li == 0)
    def _():
        st_sc[...] = jnp.zeros(st_sc.shape, F32)

    lb = nchunk * chunk
    row = lax.broadcasted_iota(jnp.int32, (chunk, chunk), 0)
    col = lax.broadcasted_iota(jnp.int32, (chunk, chunk), 1)
    tril = (row >= col).astype(BF16)
    brow = lax.broadcasted_iota(jnp.int32, (lb, lb), 0)
    bcol = lax.broadcasted_iota(jnp.int32, (lb, lb), 1)
    block_causal = (brow >= bcol) & (brow // chunk == bcol // chunk)
    gn = gn_ref[...]
    nt = (((1,), (1,)), ((), ()))
    tn = (((0,), (0,)), ((), ()))

    pre = []
    for b in range(nb):
        g = g_ref[b]
        g_hi = g.astype(BF16)
        r1 = g - g_hi.astype(F32)
        g_mid = r1.astype(BF16)
        g_lo = (r1 - g_mid.astype(F32)).astype(BF16)
        cums, tots, mids = [], [], []
        mid = max(chunk // 2 - 1, 0)
        for c in range(nchunk):
            sl = slice(c * chunk, (c + 1) * chunk)
            bc = (jnp.dot(tril, g_hi[sl], preferred_element_type=F32)
                  + jnp.dot(tril, g_mid[sl], preferred_element_type=F32)
                  + jnp.dot(tril, g_lo[sl], preferred_element_type=F32))
            cums.append(bc)
            tots.append(jnp.broadcast_to(bc[chunk - 1:chunk, :], bc.shape))
            mids.append(jnp.broadcast_to(bc[mid:mid + 1, :], bc.shape))
        bcum = jnp.concatenate(cums, axis=0)
        btot = jnp.concatenate(tots, axis=0)
        bmid = jnp.concatenate(mids, axis=0)
        q = q_ref[b].astype(F32) * qscale
        k = k_ref[b].astype(F32)
        v = v_ref[b]
        qt = (q * jnp.exp(bcum)).astype(BF16)
        qa = (q * jnp.exp(bcum - bmid)).astype(BF16)
        kt = (k * jnp.exp(bmid - bcum)).astype(BF16)
        k2 = (k * jnp.exp(btot - bcum)).astype(BF16)
        a = lax.dot_general(qa, kt, nt, preferred_element_type=F32)
        a = jnp.where(block_causal, a, 0.0)
        o_intra = jnp.dot(a.astype(BF16), v, preferred_element_type=F32)
        pre.append((qt, k2, v, o_intra, jnp.exp(btot)))

    states = [st_sc[b] for b in range(nb)]
    outs = [[] for _ in range(nb)]
    for c in range(nchunk):
        sl = slice(c * chunk, (c + 1) * chunk)
        for b in range(nb):
            qt, k2, v, o_intra, decay = pre[b]
            st = states[b]
            outs[b].append(o_intra[sl] + lax.dot_general(qt[sl], st.astype(BF16), nt,
                                                         preferred_element_type=F32))
            states[b] = st * decay[c * chunk:c * chunk + 1, :] + lax.dot_general(
                v[sl], k2[sl], tn, preferred_element_type=F32)

    for b in range(nb):
        st_sc[b] = states[b]
        o = jnp.concatenate(outs[b], axis=0)
        ms = jnp.mean(o * o, axis=-1, keepdims=True)
        rr = r_ref[b].astype(F32)
        y = o * lax.rsqrt(ms + EPS) * gn * (rr * jax.nn.sigmoid(rr))
        y_ref[b] = y.astype(y_ref.dtype)

    @pl.when(li == pl.num_programs(1) - 1)
    def _():
        s_ref[:, 0] = st_sc[...]


def prompt_gla(z, log_a, gla_g, n_heads, dk, dv, off_q, off_k, off_v, off_r):
    b, l, _ = z.shape
    chunk = math.gcd(l, GLA_CHUNK)
    lb = _tile(l, GLA_BLOCK)
    nchunk = lb // chunk
    kern = functools.partial(_gla_kernel, chunk=chunk, nchunk=nchunk, qscale=dk ** -0.5)
    oq, ok, ov, orr = off_q // dk, off_k // dk, off_v // dv, off_r // dv
    return pl.pallas_call(
        kern,
        grid=(n_heads, l // lb),
        in_specs=[
            pl.BlockSpec((b, lb, dk), lambda h, i: (0, i, oq + h)),
            pl.BlockSpec((b, lb, dk), lambda h, i: (0, i, ok + h)),
            pl.BlockSpec((b, lb, dv), lambda h, i: (0, i, ov + h)),
            pl.BlockSpec((b, lb, dv), lambda h, i: (0, i, orr + h)),
            pl.BlockSpec((b, lb, dk), lambda h, i: (0, i, h)),
            pl.BlockSpec((1, dv), lambda h, i: (0, 0)),
        ],
        out_specs=[
            pl.BlockSpec((b, lb, dv), lambda h, i: (0, i, h)),
            pl.BlockSpec((b, 1, dv, dk), lambda h, i: (0, h, 0, 0)),
        ],
        out_shape=[jax.ShapeDtypeStruct((b, l, n_heads * dv), BF16),
                   jax.ShapeDtypeStruct((b, n_heads, dv, dk), F32)],
        scratch_shapes=[pltpu.VMEM((b, dv, dk), F32)],
        compiler_params=_cparams(("parallel", "arbitrary")),
        name="gla_prompt",
    )(z, z, z, z, log_a, gla_g.reshape(1, dv).astype(F32))


def _decode_attn_kernel(pt_ref, q_ref, *refs, n_sub, n_heads, lam_init):
    k_refs = refs[:n_sub]
    v_refs = refs[n_sub:2 * n_sub]
    (bias_last_ref, own_ref, kn_ref, vn_ref, lam_ref, g_ref, o_ref, m_sc, l_sc, acc_sc) = refs[2 * n_sub:]
    s_idx = pl.program_id(1)
    last = pl.num_programs(1) - 1

    @pl.when(s_idx == 0)
    def _():
        m_sc[...] = jnp.full(m_sc.shape, NEG, F32)
        l_sc[...] = jnp.zeros(l_sc.shape, F32)
        acc_sc[...] = jnp.zeros(acc_sc.shape, F32)

    q = q_ref[0]
    nrow = q.shape[0]

    def update(kmats, vmats, biases):
        parts = []
        for kmat, bias in zip(kmats, biases):
            sp = lax.dot_general(q, kmat, (((1,), (1,)), ((), ())), preferred_element_type=F32)
            parts.append(sp if bias is None else sp + bias)
        s = parts[0] if len(parts) == 1 else jnp.concatenate(parts, axis=1)
        r = s.shape[1]
        rowh = lax.broadcasted_iota(jnp.int32, (nrow, r), 0) % n_heads
        colh = lax.broadcasted_iota(jnp.int32, (nrow, r), 1) % n_heads
        s = jnp.where(rowh == colh, s, NEG)
        m_prev = m_sc[...]
        m_new = jnp.maximum(m_prev, jnp.max(s, axis=-1, keepdims=True))
        alpha = jnp.exp2(m_prev - m_new)
        pr = jnp.exp2(s - m_new).astype(BF16)
        l_sc[...] = alpha * l_sc[...] + jnp.sum(pr.astype(F32), axis=-1, keepdims=True)
        pv = None
        off = 0
        for vmat in vmats:
            d = jnp.dot(pr[:, off:off + vmat.shape[0]], vmat, preferred_element_type=F32)
            pv = d if pv is None else pv + d
            off += vmat.shape[0]
        acc_sc[...] = alpha * acc_sc[...] + pv
        m_sc[...] = m_new

    kmats, vmats = [], []
    for u in range(n_sub):
        kp = k_refs[u][...]
        vp = v_refs[u][...]
        kmats.append(kp.reshape(kp.shape[0] * kp.shape[1], kp.shape[2]).astype(BF16))
        vmats.append(vp.reshape(vp.shape[0] * vp.shape[1], vp.shape[2]).astype(BF16))
    is_last = (s_idx == last).astype(F32)
    update(kmats, vmats, [None] * (n_sub - 1) + [bias_last_ref[...] * is_last])

    @pl.when(s_idx == last)
    def _():
        update([kn_ref[0]], [vn_ref[0]], [own_ref[...]])
        lv = lam_ref[...]
        lam = (jnp.exp(jnp.sum(lv[0:1] * lv[1:2], axis=-1, keepdims=True))
               - jnp.exp(jnp.sum(lv[2:3] * lv[3:4], axis=-1, keepdims=True)) + lam_init)
        on = acc_sc[...] / l_sc[...]
        o = on[:n_heads] - lam * on[n_heads:]
        ms = jnp.mean(o * o, axis=-1, keepdims=True)
        o_ref[0] = o * lax.rsqrt(ms + EPS) * g_ref[...] * (1.0 - lam_init)


def decode_attention(layer, q, k_new, v_new, cache_k, cache_v, page_table, bias_last, bias_own, lam_vecs,
                     diff_g, lam_init):
    nb, n_pages = page_table.shape
    _, _, page, n_heads, dv = cache_v.shape
    dh = dv // 2
    n_sub = DECODE_PAGES if n_pages % DECODE_PAGES == 0 else 1
    n_steps = n_pages // n_sub
    rows = page * n_heads
    qh = q.reshape(nb, n_heads, 2 * dh)
    half = (np.arange(2 * dh) // dh)[None, :] == np.arange(2)[:, None]
    qall = (qh[:, None, :, :] * jnp.asarray(half, F32)[None, :, None, :]).reshape(nb, 2 * n_heads, 2 * dh)
    qall = qall.astype(BF16)
    kn = k_new.reshape(nb, n_heads, 2 * dh).astype(BF16)
    vn = v_new.reshape(nb, n_heads, dv).astype(BF16)
    pt_flat = page_table.reshape(-1).astype(jnp.int32)

    def page_spec(u):
        return pl.BlockSpec((None, None, page, n_heads, dv),
                            lambda b, s, pt, u=u: (layer, pt[b * n_pages + s * n_sub + u], 0, 0, 0))

    const2 = lambda b, s, pt: (0, 0)
    grid_spec = pltpu.PrefetchScalarGridSpec(
        num_scalar_prefetch=1,
        grid=(nb, n_steps),
        in_specs=[pl.BlockSpec((1, 2 * n_heads, 2 * dh), lambda b, s, pt: (b, 0, 0))]
        + [page_spec(u) for u in range(n_sub)] + [page_spec(u) for u in range(n_sub)]
        + [pl.BlockSpec((2 * n_heads, rows), const2),
           pl.BlockSpec((2 * n_heads, 1), const2),
           pl.BlockSpec((1, n_heads, 2 * dh), lambda b, s, pt: (b, 0, 0)),
           pl.BlockSpec((1, n_heads, dv), lambda b, s, pt: (b, 0, 0)),
           pl.BlockSpec((4, dh), const2),
           pl.BlockSpec((1, dv), const2)],
        out_specs=pl.BlockSpec((1, n_heads, dv), lambda b, s, pt: (b, 0, 0)),
        scratch_shapes=[pltpu.VMEM((2 * n_heads, 1), F32), pltpu.VMEM((2 * n_heads, 1), F32),
                        pltpu.VMEM((2 * n_heads, dv), F32)],
    )
    kern = functools.partial(_decode_attn_kernel, n_sub=n_sub, n_heads=n_heads, lam_init=lam_init)
    return pl.pallas_call(
        kern, grid_spec=grid_spec,
        out_shape=jax.ShapeDtypeStruct((nb, n_heads, dv), F32),
        compiler_params=_cparams(("parallel", "arbitrary")),
        name="diff_attn_decode",
    )(pt_flat, qall, *([cache_k] * n_sub), *([cache_v] * n_sub), bias_last, bias_own, kn, vn,
      lam_vecs, diff_g.reshape(1, dv).astype(F32))


def _gla_step_kernel(s_ref, a_ref, k_ref, q_ref, v_ref, r_ref, gn_ref, *rest):
    s_out_ref, y_ref = rest[-2:]
    s_new = jnp.exp(a_ref[0, 0]) * s_ref[...] + k_ref[0, 0] * v_ref[0, 0]
    s_out_ref[...] = s_new
    o = jnp.sum(q_ref[0, 0] * s_new, axis=0, keepdims=True)
    ms = jnp.mean(o * o, axis=-1, keepdims=True)
    rr = r_ref[0, 0]
    y_ref[0, 0] = o * lax.rsqrt(ms + EPS) * gn_ref[...] * (rr * jax.nn.sigmoid(rr))


def decode_gla(layer, state, new_states, log_a, k, q, v, r, gla_g):
    _, nb, n_heads, dk, dv = state.shape
    col = lambda t: t.reshape(nb, n_heads, dk, 1)
    rowv = lambda t: t.reshape(nb, n_heads, 1, dv)
    cspec = pl.BlockSpec((1, 1, dk, 1), lambda b, h: (b, h, 0, 0))
    rspec = pl.BlockSpec((1, 1, 1, dv), lambda b, h: (b, h, 0, 0))
    sspec = pl.BlockSpec((None, None, None, dk, dv), lambda b, h: (layer, b, h, 0, 0))
    in_specs = [sspec, cspec, cspec, cspec, rspec, rspec, pl.BlockSpec((1, dv), lambda b, h: (0, 0))]
    args = [state, col(log_a), col(k), col(q), rowv(v), rowv(r), gla_g.reshape(1, dv).astype(F32)]
    aliases = {}
    if new_states is not None:
        in_specs.append(pl.BlockSpec(memory_space=pl.ANY))
        aliases[len(args)] = 0
        args.append(new_states)
    return pl.pallas_call(
        _gla_step_kernel,
        grid=(nb, n_heads),
        in_specs=in_specs,
        out_specs=[sspec, rspec],
        out_shape=[jax.ShapeDtypeStruct(state.shape, F32),
                   jax.ShapeDtypeStruct((nb, n_heads, 1, dv), F32)],
        input_output_aliases=aliases,
        compiler_params=_cparams(("parallel", "parallel")),
        name="gla_decode",
    )(*args)


def _split_w_in(w_in, sizes):
    offs = [int(o) for o in np.concatenate([[0], np.cumsum(sizes)])]
    depth = w_in.shape[0]
    main = [w_in[i, :, :offs[7]].astype(BF16) for i in range(depth)]
    gates = [w_in[i, :, offs[8]:offs[10]].astype(BF16) for i in range(depth)]
    glow = [w_in[i, :, offs[7]:offs[8]].astype(BF16) for i in range(depth)]
    return offs, main, gates, glow


def _layer_weights(i, split, sizes, others):
    offs, main, gates, glow = split
    w = {"q": (main[i], None, offs[0], sizes[0]), "k": (main[i], None, offs[1], sizes[1]),
         "v": (main[i], None, offs[2], sizes[2]), "rest": (main[i], None, offs[3], offs[7] - offs[3]),
         "gates": (gates[i], None, 0, offs[10] - offs[8]), "glow": (glow[i], None, 0, sizes[7])}
    for name, arr in others.items():
        w[name] = (arr, i, 0, arr.shape[2])
    return w


def _pair(a_idx, wspec):
    return (a_idx, wspec[0], wspec[1], wspec[2])


def _dense_front(xp, xs, lw, g_mix, q_g, k_g, gate_w2, gate_b, dh, stacks):
    hp = rmsnorm_rows(xp, g_mix)
    hs = rmsnorm_rows(xs, g_mix)
    layer, depth, kbuf, vbuf = stacks
    d_qk = lw["q"][3]
    d_v = lw["v"][3]
    qn = functools.partial(_ep_groupnorm, group=dh, scale=dh ** -0.5 * LOG2E)
    kn = functools.partial(_ep_groupnorm, group=dh, scale=1.0)
    gq = (q_g.reshape(1, dh).astype(F32), "full", 0)
    gk = (k_g.reshape(1, dh).astype(F32), "full", 0)
    tm = ROW_TILE
    qt, q_s = fused_matmul([hp], [_pair(0, lw["q"])], d_qk, [gq], [(d_qk, BF16, "t")], qn, tm=tm,
                           small=([hs], [gq], [(d_qk, BF16, "n")]), name="in_q")
    kbuf, k16, k_s = fused_matmul([hp], [_pair(0, lw["k"])], d_qk, [gk], [(d_qk, F32, "n"), (d_qk, BF16, "n")],
                                  kn, tm=tm, stacked=(0, layer, depth, kbuf),
                                  small=([hs], [gk], [(d_qk, F32, "n")]), name="in_k")
    vbuf, vt, v_s = fused_matmul([hp], [_pair(0, lw["v"])], d_v, [], [(d_v, F32, "n"), (d_v, BF16, "t")],
                                 _ep_cast, tm=tm, stacked=(0, layer, depth, vbuf),
                                 small=([hs], [], [(d_v, F32, "n")]), name="in_v")
    n_rest = lw["rest"][3]
    rest, rest_s = fused_matmul([hp], [_pair(0, lw["rest"])], n_rest, [], [(n_rest, BF16, "n")], _ep_cast,
                                tm=tm, small=([hs], [], [(n_rest, BF16, "n")]), name="in_rest")
    n_gates = lw["gates"][3]
    gates, gates_s = fused_matmul([hp], [_pair(0, lw["gates"])], n_gates, [], [(n_gates, BF16, "n")], _ep_cast,
                                  tm=tm, small=([hs], [], [(n_gates, BF16, "n")]), name="in_gates")
    n_gate = gate_w2.shape[1]
    gate_extras = [(gate_w2.astype(F32), "full", 0), (gate_b.reshape(1, n_gate).astype(F32), "full", 0)]
    log_a, log_a_s = fused_matmul([hp], [_pair(0, lw["glow"])], lw["glow"][3], gate_extras,
                                  [(n_gate, F32, "whole")], _ep_gate, tm=tm,
                                  small=([hs], gate_extras, [(n_gate, F32, "whole")]), name="in_gate")
    return ((qt, kbuf, k16, vbuf, vt, rest, gates, log_a), (q_s, k_s, v_s, rest_s, gates_s, log_a_s))


def _dense_back(xp, xs, y_p, y_s, gates_p, gates_s, p_p, p_s, lw, g_ffn, g_ple):
    d = xp.shape[1]
    tm = ROW_TILE
    tn = _tile(d, COL_TILE_NARROW)
    gate_tiles = lambda g: [(g, "tile", 0), (g, "tile", d // tn)]
    merged, merged_s = fused_matmul(list(y_p), [_pair(0, lw["w_pa"]), _pair(1, lw["w_pb"])], d,
                                    gate_tiles(gates_p), [(d, BF16, "n")], _ep_merge, tm=tm, tn=tn,
                                    small=(list(y_s), gate_tiles(gates_s), [(d, BF16, "n")]), name="merge")
    xp, xs = fused_matmul([merged], [_pair(0, lw["w_o"])], d, [(xp, "tile", 0)], [(d, F32, "n")],
                          _ep_residual, tm=tm, tn=2 * tn,
                          small=([merged_s], [(xs, "tile", 0)], [(d, F32, "n")]), name="out_proj")
    hp = rmsnorm_rows(xp, g_ffn)
    hs = rmsnorm_rows(xs, g_ffn)
    d_ff = lw["w_gate"][3]
    gu, gu_s = fused_matmul([hp], [_pair(0, lw["w_gate"]), _pair(0, lw["w_up"])], d_ff, [], [(d_ff, BF16, "n")],
                            _ep_swiglu, tm=tm, tn=COL_TILE_NARROW,
                            small=([hs], [], [(d_ff, BF16, "n")]), name="ffn_up")
    xp, xs = fused_matmul([gu], [_pair(0, lw["w_down"])], d, [(xp, "tile", 0)], [(d, F32, "n")], _ep_residual,
                          tm=tm // 2, tn=COL_TILE_NARROW,
                          small=([gu_s], [(xs, "tile", 0)], [(d, F32, "n")]), name="ffn_down")
    hp = rmsnorm_rows(xp, g_ple)
    hs = rmsnorm_rows(xs, g_ple)
    xp, xs = fused_matmul([hp, p_p], [_pair(0, lw["w_ple_gate"]), _pair(1, lw["w_ple_proj"])], d,
                          [(xp, "tile", 0)], [(d, F32, "n")], _ep_ple, tm=tm, tn=tn,
                          small=([hs, p_s], [(xs, "tile", 0)], [(d, F32, "n")]), name="ple")
    return xp, xs


def kernel(x_prompt, x_sample, cache_k, cache_v, state_gla, page_table, p_prompt, p_sample, rel_bias, norm_mix_g, w_in, gla_gate_w2, gla_gate_b, q_norm_g, k_norm_g, lambda_q1, lambda_k1, lambda_q2, lambda_k2, diff_norm_g, gla_norm_g, w_pa, w_pb, w_o, norm_ffn_g, w_gate, w_up, w_down, norm_ple_g, w_ple_gate, w_ple_proj):
    depth = w_in.shape[0]
    bp, lp, d = x_prompt.shape
    bs, ls, _ = x_sample.shape
    assert ls == 1
    _, _, page, h_a, dv_a = cache_v.shape
    dh_a = dv_a // 2
    d_a = h_a * dv_a
    _, _, h_b, dk_b, dv_b = state_gla.shape
    d_b = h_b * dv_b
    rank = gla_gate_w2.shape[1]
    sizes = [d_a, d_a, d_a, h_b * dk_b, h_b * dk_b, d_b, d_b, rank, d, d]
    off_qb, off_kb = 0, h_b * dk_b
    off_vb = 2 * h_b * dk_b
    off_r = off_vb + d_b
    others = {"w_pa": w_pa, "w_pb": w_pb, "w_o": w_o, "w_gate": w_gate, "w_up": w_up, "w_down": w_down,
              "w_ple_gate": w_ple_gate, "w_ple_proj": w_ple_proj}

    mp = bp * lp
    ms = SAMPLE_ROWS
    xp = x_prompt.reshape(mp, d)
    xs = jnp.zeros((ms, d), F32).at[:bs].set(x_sample.reshape(bs, d))
    attn_bias = _prompt_bias_tiles(rel_bias, _tile(lp, ATTN_TILE))
    dec_bias_last, dec_bias_own = _decode_bias(rel_bias, page, h_a)
    split = _split_w_in(w_in, sizes)
    p_prompt3 = p_prompt.reshape(depth, mp, -1)
    p_sample3 = jnp.zeros((depth, ms, p_sample.shape[-1]), F32).at[:, :bs].set(p_sample.reshape(depth, bs, -1))
    kbuf = vbuf = sbuf = None

    sp_l, ks_l, vs_l = [], [], []
    for i in range(depth):
        lam_init = 0.8 - 0.6 * math.exp(-0.3 * i)
        lw = _layer_weights(i, split, sizes, others)
        lam_vecs = jnp.stack([lambda_q1[i], lambda_k1[i], lambda_q2[i], lambda_k2[i]]).astype(F32)

        (qt, kbuf, k16, vbuf, vt, rest, gates, log_a), (q_s, k_s, v_s, rest_s, gates_s, log_a_s) = _dense_front(
            xp, xs, lw, norm_mix_g[i], q_norm_g[i], k_norm_g[i], gla_gate_w2[i], gla_gate_b[i], dh_a,
            (i, depth, kbuf, vbuf))

        to3 = lambda t: t.reshape(bp, lp, t.shape[1])
        ya = prompt_attention(qt, to3(k16), vt, attn_bias, lam_vecs, diff_norm_g[i], lam_init, h_a, bp, lp)
        yb, st = prompt_gla(to3(rest), to3(log_a), gla_norm_g[i], h_b, dk_b, dv_b, off_qb, off_kb, off_vb, off_r)
        sp_l.append(jnp.swapaxes(st, 2, 3))

        ya_s = decode_attention(i, q_s[:bs].astype(F32), k_s[:bs], v_s[:bs], cache_k, cache_v, page_table,
                                dec_bias_last, dec_bias_own, lam_vecs, diff_norm_g[i], lam_init)
        rest32 = rest_s[:bs].astype(F32)
        hk = lambda t: t.reshape(bs, h_b, -1)
        sbuf, yb_s = decode_gla(i, state_gla, sbuf, hk(log_a_s[:bs]),
                                hk(rest32[:, off_kb:off_kb + h_b * dk_b]),
                                hk(rest32[:, off_qb:off_qb + h_b * dk_b]) * (dk_b ** -0.5),
                                hk(rest32[:, off_vb:off_vb + d_b]), hk(rest32[:, off_r:off_r + d_b]),
                                gla_norm_g[i])
        pad = lambda t: jnp.zeros((ms, t.shape[1]), BF16).at[:bs].set(t.astype(BF16))

        xp, xs = _dense_back(xp, xs, (ya.reshape(mp, d_a), yb.reshape(mp, d_b)),
                             (pad(ya_s.reshape(bs, d_a)), pad(yb_s.reshape(bs, d_b))), gates, gates_s,
                             (p_prompt3, i), (p_sample3, i), lw, norm_ffn_g[i], norm_ple_g[i])
        ks_l.append(k_s[:bs].reshape(bs, 1, h_a, dv_a))
        vs_l.append(v_s[:bs].reshape(bs, 1, h_a, dv_a))

    return (xp.reshape(bp, lp, d), xs[:bs].reshape(bs, 1, d),
            kbuf.reshape(depth, bp, lp, h_a, dv_a), vbuf.reshape(depth, bp, lp, h_a, dv_a),
            jnp.stack(sp_l), jnp.stack(ks_l), jnp.stack(vs_l), sbuf)
```
